```python
import math
import jax, jax.numpy as jnp
from jax import lax
import numpy as np

D_MODEL = 2048
BATCH = 4
SEQ = 2048
DEPTH = 4

D_MIX = D_MODEL
ATTN_WIDTH = D_MIX // 2
ATTN_HEAD_DIM = 128
ATTN_HEADS = ATTN_WIDTH // ATTN_HEAD_DIM
ROPE_DIM = ATTN_HEAD_DIM // 4
ROPE_THETA = 500000.0
MOBA_BLOCK = 256
MOBA_TOPK = 3
Q_CHUNK = 16

SSD_WIDTH = D_MIX - ATTN_WIDTH
SSD_HEAD_DIM = 64
SSD_HEADS = SSD_WIDTH // SSD_HEAD_DIM
SSD_GROUPS = 2
SSD_STATE = 128
SSD_CONV = 4
SSD_CHUNK = 128
CONV_DIM = SSD_WIDTH + 2 * SSD_GROUPS * SSD_STATE
IN_DIM = 3 * ATTN_WIDTH + SSD_WIDTH + CONV_DIM + SSD_HEADS

D_FF = 4 * D_MODEL
NORM_EPS = 1e-6

kernel_name = "hymba_moba_ssd_sandwich_trunk"


def rms_norm(x, gain):
    xf = x.astype(jnp.float32)
    inv = lax.rsqrt(jnp.mean(xf * xf, axis=-1, keepdims=True) + NORM_EPS)
    return (xf * inv).astype(x.dtype) * gain


def partial_rope(t, positions):
    half = ROPE_DIM // 2
    inv_freq = ROPE_THETA ** (-jnp.arange(half, dtype=jnp.float32) / half)
    ang = positions.astype(jnp.float32)[:, None, :, None] * inv_freq
    cos, sin = jnp.cos(ang), jnp.sin(ang)
    x1 = t[..., :half]
    x2 = t[..., half:ROPE_DIM]
    rot = jnp.concatenate([x1 * cos - x2 * sin, x2 * cos + x1 * sin], axis=-1).astype(t.dtype)
    return jnp.concatenate([rot, t[..., ROPE_DIM:]], axis=-1)


def moba_attention(q, k, v, positions):
    B_, H, S, Dh = q.shape
    n_blocks = -(-S // MOBA_BLOCK)
    pad = n_blocks * MOBA_BLOCK - S
    q = partial_rope(q, positions) * (Dh ** -0.5)
    k = partial_rope(k, positions)
    kb = jnp.pad(k, ((0, 0), (0, 0), (0, pad), (0, 0))).reshape(B_, H, n_blocks, MOBA_BLOCK, Dh)
    vb = jnp.pad(v, ((0, 0), (0, 0), (0, pad), (0, 0))).reshape(B_, H, n_blocks, MOBA_BLOCK, Dh)
    k_mean = jnp.mean(kb.astype(jnp.float32), axis=3)
    gate = jnp.einsum('bhsd,bhnd->bhsn', q.astype(jnp.float32), k_mean)
    q_block = jnp.arange(S) // MOBA_BLOCK
    past = jnp.arange(n_blocks)[None, :] < q_block[:, None]
    gate = jnp.where(past, gate, -jnp.inf)
    k_sel = min(MOBA_TOPK, n_blocks)
    _, idx = lax.top_k(gate, k_sel)
    valid = idx < q_block[None, None, :, None]

    n_chunks = S // Q_CHUNK
    q_c = q.reshape(B_, H, n_chunks, Q_CHUNK, Dh).transpose(2, 0, 1, 3, 4)
    idx_c = idx.reshape(B_, H, n_chunks, Q_CHUNK, k_sel).transpose(2, 0, 1, 3, 4)
    val_c = valid.reshape(B_, H, n_chunks, Q_CHUNK, k_sel).transpose(2, 0, 1, 3, 4)
    starts = jnp.arange(n_chunks, dtype=jnp.int32) * Q_CHUNK
    b_ix = jnp.arange(B_)[:, None, None, None]
    h_ix = jnp.arange(H)[None, :, None, None]
    key_off = jnp.arange(MOBA_BLOCK)

    def attend_chunk(args):
        qc, ic, vc, start = args
        own = start // MOBA_BLOCK
        k_g = kb[b_ix, h_ix, ic]
        v_g = vb[b_ix, h_ix, ic]
        k_own = lax.dynamic_index_in_dim(kb, own, axis=2, keepdims=False)
        v_own = lax.dynamic_index_in_dim(vb, own, axis=2, keepdims=False)
        s_sel = jnp.einsum('bhqd,bhqjkd->bhqjk', qc, k_g).astype(jnp.float32)
        s_sel = jnp.where(vc[..., None], s_sel, -jnp.inf).reshape(B_, H, Q_CHUNK, k_sel * MOBA_BLOCK)
        s_own = jnp.einsum('bhqd,bhkd->bhqk', qc, k_own).astype(jnp.float32)
        q_pos = start + jnp.arange(Q_CHUNK)
        k_pos = own * MOBA_BLOCK + key_off
        s_own = jnp.where(k_pos[None, :] <= q_pos[:, None], s_own, -jnp.inf)
        p = jax.nn.softmax(jnp.concatenate([s_sel, s_own], axis=-1), axis=-1).astype(v.dtype)
        p_sel = p[..., :k_sel * MOBA_BLOCK].reshape(B_, H, Q_CHUNK, k_sel, MOBA_BLOCK)
        p_own = p[..., k_sel * MOBA_BLOCK:]
        return (jnp.einsum('bhqjk,bhqjkd->bhqd', p_sel, v_g)
                + jnp.einsum('bhqk,bhkd->bhqd', p_own, v_own))

    out = lax.map(attend_chunk, (q_c, idx_c, val_c, starts))
    return out.transpose(1, 2, 0, 3, 4).reshape(B_, H, S, Dh)


def causal_depthwise_conv(x, w, b):
    K, C = w.shape
    out = lax.conv_general_dilated(x, w[:, None, :], window_strides=(1,), padding=[(K - 1, 0)],
                                   dimension_numbers=('NWC', 'WIO', 'NWC'), feature_group_count=C)
    return out + b


def segsum(x):
    T = x.shape[-1]
    xr = jnp.broadcast_to(x[..., :, None], x.shape + (T,))
    xr = jnp.where(jnp.tril(jnp.ones((T, T), dtype=bool), -1), xr, 0.0)
    cs = jnp.cumsum(xr, axis=-2)
    return jnp.where(jnp.tril(jnp.ones((T, T), dtype=bool)), cs, -jnp.inf)


def ssd_scan(x, a_dt, Bm, Cm):
    B_, S, H, P = x.shape
    G, N = Bm.shape[2], Bm.shape[3]
    E = H // G
    L = SSD_CHUNK
    nc = S // L
    xc = x.reshape(B_, nc, L, G, E, P)
    Bc = Bm.reshape(B_, nc, L, G, N)
    Cc = Cm.reshape(B_, nc, L, G, N)
    a = a_dt.reshape(B_, nc, L, G, E).transpose(0, 3, 4, 1, 2)
    a_cs = jnp.cumsum(a, axis=-1)
    decay_in = jnp.exp(segsum(a)).astype(x.dtype)
    cb = jnp.einsum('bclgn,bcsgn->bcgls', Cc, Bc)
    y_diag = jnp.einsum('bcgls,bgecls,bcsgep->bclgep', cb, decay_in, xc)
    decay_to_end = jnp.exp(a_cs[..., -1:] - a_cs).astype(x.dtype)
    chunk_states = jnp.einsum('bclgn,bgecl,bclgep->bcgepn', Bc, decay_to_end, xc)
    chunk_decay = jnp.exp(a_cs[..., -1]).astype(x.dtype)

    def step(state, inp):
        dec, new = inp
        return state * dec[..., None, None] + new, state

    init = jnp.zeros((B_, G, E, P, N), x.dtype)
    _, prev = lax.scan(step, init, (chunk_decay.transpose(3, 0, 1, 2),
                                    chunk_states.transpose(1, 0, 2, 3, 4, 5)))
    prev = prev.transpose(1, 0, 2, 3, 4, 5)
    y_off = jnp.einsum('bclgn,bcgepn,bgecl->bclgep', Cc, prev, jnp.exp(a_cs).astype(x.dtype))
    return (y_diag + y_off).reshape(B_, S, H, P)


def hybrid_mixer(h, positions, w_in, conv_w, conv_b, dt_bias, a_log, d_skip,
                 attn_out_norm, ssd_out_norm, w_out):
    B_, S, _ = h.shape
    proj = h @ w_in
    o1 = ATTN_WIDTH
    o2 = 2 * ATTN_WIDTH
    o3 = 3 * ATTN_WIDTH
    o4 = o3 + SSD_WIDTH
    o5 = o4 + CONV_DIM
    q, k, v, z, xbc, dt = jnp.split(proj, [o1, o2, o3, o4, o5], axis=-1)

    def heads(t):
        return t.reshape(B_, S, ATTN_HEADS, ATTN_HEAD_DIM).transpose(0, 2, 1, 3)
    attn = moba_attention(heads(q), heads(k), heads(v), positions)
    attn = attn.transpose(0, 2, 1, 3).reshape(B_, S, ATTN_WIDTH)
    attn = rms_norm(attn, attn_out_norm)

    xbc = jax.nn.silu(causal_depthwise_conv(xbc, conv_w, conv_b))
    xs, Bm, Cm = jnp.split(xbc, [SSD_WIDTH, SSD_WIDTH + SSD_GROUPS * SSD_STATE], axis=-1)
    dt = jax.nn.softplus(dt.astype(jnp.float32) + dt_bias.astype(jnp.float32))
    a = -jnp.exp(a_log.astype(jnp.float32))
    xh = xs.reshape(B_, S, SSD_HEADS, SSD_HEAD_DIM)
    y = ssd_scan(xh * dt[..., None].astype(xh.dtype), dt * a,
                 Bm.reshape(B_, S, SSD_GROUPS, SSD_STATE), Cm.reshape(B_, S, SSD_GROUPS, SSD_STATE))
    y = y + d_skip[:, None] * xh
    y = y.reshape(B_, S, SSD_WIDTH) * jax.nn.silu(z)
    y = rms_norm(y.reshape(B_, S, SSD_GROUPS, SSD_WIDTH // SSD_GROUPS),
                 ssd_out_norm.reshape(SSD_GROUPS, SSD_WIDTH // SSD_GROUPS)).reshape(B_, S, SSD_WIDTH)

    mix = jnp.concatenate([attn, y], axis=-1)
    return mix @ w_out


def setup_inputs(seed: int = 0) -> dict:
    key = jax.random.key(seed)
    ks = jax.random.split(key, 18)
    f32 = jnp.float32

    def normal(k, shape, scale):
        return jax.random.normal(k, shape, f32) * scale

    def gain(k, shape):
        return 1.0 + 0.05 * jax.random.normal(k, shape, f32)

    x = jax.random.normal(ks[0], (BATCH, SEQ, D_MODEL), f32)
    offset = jax.random.randint(ks[1], (BATCH, 1), 0, 4096, dtype=jnp.int32)
    positions = offset + jnp.arange(SEQ, dtype=jnp.int32)[None, :]
    dt_init = jnp.exp(jax.random.uniform(ks[9], (DEPTH, SSD_HEADS), f32,
                                         math.log(1e-3), math.log(1e-1)))
    dt_bias = dt_init + jnp.log(-jnp.expm1(-dt_init))
    a_log = jnp.log(jax.random.uniform(ks[10], (DEPTH, SSD_HEADS), f32, 1.0, 16.0))
    return {
        "x": x,
        "positions": positions,
        "pre_mix_norm": gain(ks[2], (DEPTH, D_MODEL)),
        "post_mix_norm": gain(ks[3], (DEPTH, D_MODEL)),
        "pre_mlp_norm": gain(ks[4], (DEPTH, D_MODEL)),
        "post_mlp_norm": gain(ks[5], (DEPTH, D_MODEL)),
        "w_in": normal(ks[6], (DEPTH, D_MODEL, IN_DIM), D_MODEL ** -0.5),
        "conv_w": normal(ks[7], (DEPTH, SSD_CONV, CONV_DIM), SSD_CONV ** -0.5),
        "conv_b": normal(ks[8], (DEPTH, CONV_DIM), 0.01),
        "dt_bias": dt_bias,
        "a_log": a_log,
        "d_skip": gain(ks[11], (DEPTH, SSD_HEADS)),
        "attn_out_norm": gain(ks[12], (DEPTH, ATTN_WIDTH)),
        "ssd_out_norm": gain(ks[13], (DEPTH, SSD_WIDTH)),
        "w_out": normal(ks[14], (DEPTH, D_MIX, D_MODEL), D_MIX ** -0.5),
        "w_up": normal(ks[15], (DEPTH, D_MODEL, D_FF), D_MODEL ** -0.5),
        "w_down": normal(ks[16], (DEPTH, D_FF, D_MODEL), D_FF ** -0.5),
    }


def reference(x, positions, pre_mix_norm, post_mix_norm, pre_mlp_norm, post_mlp_norm,
              w_in, conv_w, conv_b, dt_bias, a_log, d_skip, attn_out_norm, ssd_out_norm,
              w_out, w_up, w_down):
    for l in range(DEPTH):
        h = rms_norm(x, pre_mix_norm[l])
        h = hybrid_mixer(h, positions, w_in[l], conv_w[l], conv_b[l], dt_bias[l], a_log[l],
                         d_skip[l], attn_out_norm[l], ssd_out_norm[l], w_out[l])
        x = x + rms_norm(h, post_mix_norm[l])
        h = rms_norm(x, pre_mlp_norm[l])
        h = jnp.square(jax.nn.relu(h @ w_up[l])) @ w_down[l]
        x = x + rms_norm(h, post_mlp_norm[l])
    return x
```

```python
import functools
import math

import jax
import jax.numpy as jnp
from jax import lax
from jax.experimental import pallas as pl
from jax.experimental.pallas import tpu as pltpu

F32 = jnp.float32
BF16 = jnp.bfloat16

D_MODEL = 2048
ATTN_WIDTH = 1024
HEAD_DIM = 128
ATTN_HEADS = ATTN_WIDTH // HEAD_DIM
ROPE_DIM = HEAD_DIM // 4
ROPE_HALF = ROPE_DIM // 2
ROPE_THETA = 500000.0
MOBA_BLOCK = 256
MOBA_TOPK = 3

SSD_WIDTH = 1024
SSD_HEAD_DIM = 64
SSD_HEADS = SSD_WIDTH // SSD_HEAD_DIM
SSD_GROUPS = 2
SSD_STATE = 128
SSD_CONV = 4
SSD_CHUNK = 128
SSD_BC = 2 * SSD_GROUPS * SSD_STATE
CONV_DIM = SSD_WIDTH + SSD_BC
IN_DIM = 3 * ATTN_WIDTH + SSD_WIDTH + CONV_DIM + SSD_HEADS
LANES = 128
SUBLANES = 8
IN_PAD = -(-IN_DIM // LANES) * LANES
D_FF = 4 * D_MODEL
NORM_EPS = 1e-6

OFF_Z = 3 * ATTN_WIDTH
OFF_X = OFF_Z + SSD_WIDTH
OFF_BC = OFF_X + SSD_WIDTH
OFF_DT = OFF_BC + SSD_BC

VMEM_LIMIT = 56 * 1024 * 1024

NT_DIMS = (((1,), (1,)), ((), ()))


def _rms_scale(v):
    return lax.rsqrt(jnp.mean(v * v, axis=-1, keepdims=True) + NORM_EPS)


def _silu(v):
    return v * jax.nn.sigmoid(v)


def _rope_table_kernel(pos_ref, c_ref, sa_ref, sb_ref):
    pos = pos_ref[...].astype(F32)
    lane = lax.broadcasted_iota(jnp.int32, (1, LANES), 1)
    k = (lane & (ROPE_HALF - 1)).astype(F32)
    inv_freq = jnp.exp(k * (-math.log(ROPE_THETA) / ROPE_HALF))
    ang = pos * inv_freq
    cos = jnp.cos(ang)
    sin = jnp.sin(ang)
    in_rope = lane < ROPE_DIM
    c_ref[...] = jnp.where(in_rope, cos, 1.0)
    sa_ref[...] = jnp.where(lane < ROPE_HALF, -sin, 0.0)
    sb_ref[...] = jnp.where(jnp.logical_and(lane >= ROPE_HALF, in_rope), sin, 0.0)


def _rope_tables(positions):
    B, S = positions.shape
    spec = pl.BlockSpec((None, S, LANES), lambda b: (b, 0, 0))
    shape = jax.ShapeDtypeStruct((B, S, LANES), F32)
    return pl.pallas_call(
        _rope_table_kernel,
        grid=(B,),
        in_specs=[pl.BlockSpec((None, S, 1), lambda b: (b, 0, 0))],
        out_specs=[spec, spec, spec],
        out_shape=[shape, shape, shape],
        name="rope_tables",
    )(positions.reshape(B, S, 1))


def _norm_matmul_kernel(x_ref, g_ref, w_ref, o_ref, h_ref):
    @pl.when(pl.program_id(1) == 0)
    def _():
        x = x_ref[...]
        h_ref[...] = ((x * _rms_scale(x)) * g_ref[...]).astype(BF16)

    o_ref[...] = jnp.dot(h_ref[...], w_ref[...], preferred_element_type=F32)


def _norm_matmul(x, gain, w, layer, *, tm=512, tn=1920):
    T, D = x.shape
    N = w.shape[-1]
    return pl.pallas_call(
        _norm_matmul_kernel,
        grid=(T // tm, N // tn),
        in_specs=[
            pl.BlockSpec((tm, D), lambda i, j: (i, 0)),
            pl.BlockSpec((None, 1, D), lambda i, j: (layer, 0, 0)),
            pl.BlockSpec((None, D, tn), lambda i, j: (layer, 0, j)),
        ],
        out_specs=pl.BlockSpec((tm, tn), lambda i, j: (i, j)),
        out_shape=jax.ShapeDtypeStruct((T, N), F32),
        scratch_shapes=[pltpu.VMEM((tm, D), BF16)],
        compiler_params=pltpu.CompilerParams(
            dimension_semantics=("parallel", "arbitrary"), vmem_limit_bytes=VMEM_LIMIT),
        name="norm_in_proj",
    )(x, gain, w)


def _moba_kernel(q_ref, k_ref, v_ref, c_ref, sa_ref, sb_ref, o_ref,
                 kr_ref, vt_ref, km_ref, sel_ref, *, n_blocks):
    i = pl.program_id(2)
    blk = MOBA_BLOCK

    def rope(t, rows):
        return (t * c_ref[rows, :]
                + pltpu.roll(t, LANES - ROPE_HALF, 1) * sa_ref[rows, :]
                + pltpu.roll(t, ROPE_HALF, 1) * sb_ref[rows, :])

    @pl.when(i == 0)
    def _():
        for b in range(n_blocks):
            rows = pl.ds(b * blk, blk)
            kb = rope(k_ref[rows, :], rows)
            kr_ref[b] = kb.astype(BF16)
            km_ref[b:b + 1, :] = jnp.mean(kb, axis=0, keepdims=True)
            vt_ref[b] = v_ref[rows, :].T.astype(BF16)

    q = rope(q_ref[...], pl.ds(pl.multiple_of(i * blk, blk), blk)) * (HEAD_DIM ** -0.5)

    gate = lax.dot_general(km_ref[...], q, NT_DIMS, precision=lax.Precision.HIGHEST,
                           preferred_element_type=F32)
    jidx = lax.broadcasted_iota(jnp.int32, gate.shape, 0)
    past = jidx < i
    gate = jnp.where(past, gate, -jnp.inf)
    rank = jnp.zeros(gate.shape, jnp.int32)
    for m in range(n_blocks):
        gm = gate[m:m + 1, :]
        beats = jnp.logical_or(gm > gate, jnp.logical_and(gm == gate, m < jidx))
        rank = rank + beats.astype(jnp.int32)
    sel_ref[...] = jnp.logical_and(past, rank < MOBA_TOPK).astype(F32)

    qb = q.astype(BF16)
    s = lax.dot_general(kr_ref[i], qb, NT_DIMS, preferred_element_type=F32)
    key_i = lax.broadcasted_iota(jnp.int32, s.shape, 0)
    qry_i = lax.broadcasted_iota(jnp.int32, s.shape, 1)
    s = jnp.where(key_i <= qry_i, s, -jnp.inf)
    m0 = jnp.max(s, axis=0, keepdims=True)
    p = jnp.exp(s - m0)
    l0 = jnp.sum(p, axis=0, keepdims=True)
    acc0 = jnp.dot(vt_ref[i], p.astype(BF16), preferred_element_type=F32)

    def body(j, carry):
        m_prev, l_prev, acc = carry
        sj = lax.dot_general(kr_ref[j], qb, NT_DIMS, preferred_element_type=F32)
        sj = jnp.where(sel_ref[pl.ds(j, 1), :] > 0.0, sj, -jnp.inf)
        m_new = jnp.maximum(m_prev, jnp.max(sj, axis=0, keepdims=True))
        alpha = jnp.exp(m_prev - m_new)
        pj = jnp.exp(sj - m_new)
        l_new = alpha * l_prev + jnp.sum(pj, axis=0, keepdims=True)
        acc = alpha * acc + jnp.dot(vt_ref[j], pj.astype(BF16), preferred_element_type=F32)
        return m_new, l_new, acc

    _, l_fin, acc = lax.fori_loop(0, i, body, (m0, l0, acc0))
    o_ref[...] = (acc / l_fin).T


def _moba_attention(proj, c_tab, sa_tab, sb_tab):
    B, S, _ = proj.shape
    nb = S // MOBA_BLOCK
    H = ATTN_HEADS
    full = lambda off: pl.BlockSpec((None, S, HEAD_DIM), lambda b, h, i: (b, 0, off + h))
    tab = pl.BlockSpec((None, S, LANES), lambda b, h, i: (b, 0, 0))
    return pl.pallas_call(
        functools.partial(_moba_kernel, n_blocks=nb),
        grid=(B, H, nb),
        in_specs=[
            pl.BlockSpec((None, MOBA_BLOCK, HEAD_DIM), lambda b, h, i: (b, i, h)),
            full(H), full(2 * H), tab, tab, tab,
        ],
        out_specs=pl.BlockSpec((None, MOBA_BLOCK, HEAD_DIM), lambda b, h, i: (b, i, h)),
        out_shape=jax.ShapeDtypeStruct((B, S, ATTN_WIDTH), F32),
        scratch_shapes=[
            pltpu.VMEM((nb, MOBA_BLOCK, HEAD_DIM), BF16),
            pltpu.VMEM((nb, HEAD_DIM, MOBA_BLOCK), BF16),
            pltpu.VMEM((nb, HEAD_DIM), F32),
            pltpu.VMEM((nb, MOBA_BLOCK), F32),
        ],
        compiler_params=pltpu.CompilerParams(
            dimension_semantics=("parallel", "parallel", "arbitrary"), vmem_limit_bytes=VMEM_LIMIT),
        name="moba_attention",
    )(proj, proj, proj, c_tab, sa_tab, sb_tab)


def _ssd_kernel(z_ref, xs_ref, bc_ref, dt_ref, wx_ref, bx_ref, wbc_ref, bbc_ref,
                dtb_ref, alog_ref, dskip_ref, gn_ref, o_ref,
                extx_ref, extbc_ref, state_ref):
    L = SSD_CHUNK
    N = SSD_STATE
    halo = SUBLANES

    @pl.when(pl.program_id(1) == 0)
    def _():
        extx_ref[0:halo, :] = jnp.zeros((halo, SSD_WIDTH), F32)
        extbc_ref[0:halo, :] = jnp.zeros((halo, SSD_BC), F32)
        state_ref[...] = jnp.zeros(state_ref.shape, F32)

    def conv_silu(ext_ref, cur_ref, w_ref, b_ref):
        ext_ref[halo:halo + L, :] = cur_ref[...]
        acc = b_ref[...] + ext_ref[pl.ds(halo, L), :] * w_ref[SSD_CONV - 1:SSD_CONV, :]
        for k in range(SSD_CONV - 1):
            acc = acc + ext_ref[pl.ds(halo - (SSD_CONV - 1) + k, L), :] * w_ref[k:k + 1, :]
        ext_ref[0:halo, :] = ext_ref[L:L + halo, :]
        return _silu(acc)

    xc = conv_silu(extx_ref, xs_ref, wx_ref, bx_ref)
    bcc = conv_silu(extbc_ref, bc_ref, wbc_ref, bbc_ref)

    dt_in = dt_ref[...] + dtb_ref[...]
    dt = jnp.maximum(dt_in, 0.0) + jnp.log(1.0 + jnp.exp(-jnp.abs(dt_in)))
    a_dt = dt * (-jnp.exp(alog_ref[...]))
    row = lax.broadcasted_iota(jnp.int32, (L, L), 0)
    col = lax.broadcasted_iota(jnp.int32, (L, L), 1)
    tri = row >= col
    a_cs = jnp.dot(tri.astype(F32), a_dt, precision=lax.Precision.HIGHEST,
                   preferred_element_type=F32)
    a_cs_t = a_cs.T[0:SSD_HEADS, :]
    dt_t = dt.T[0:SSD_HEADS, :]
    a_last = a_cs_t[:, L - 1:L]
    w_t = dt_t * jnp.exp(a_last - a_cs_t)
    chunk_decay = jnp.broadcast_to(jnp.exp(a_last), (SSD_HEADS, LANES))

    lane = lax.broadcasted_iota(jnp.int32, (1, LANES), 1)
    lo = lane < SSD_HEAD_DIM

    def block_diag(v):
        return jnp.concatenate([jnp.where(lo, v, 0.0), jnp.where(lo, 0.0, v)], axis=0).astype(BF16)

    heads_per_group = SSD_HEADS // SSD_GROUPS
    y_parts = []
    for g in range(SSD_GROUPS):
        b_g = bcc[:, g * N:(g + 1) * N]
        c_g = bcc[:, (SSD_GROUPS + g) * N:(SSD_GROUPS + g + 1) * N]
        cb = lax.dot_general(c_g.astype(BF16), b_g.astype(BF16), NT_DIMS,
                             preferred_element_type=F32)
        b_t = b_g.T
        for pr in range(heads_per_group // 2):
            m_parts, cs_parts, bw_parts = [], [], []
            for e in (g * heads_per_group + 2 * pr, g * heads_per_group + 2 * pr + 1):
                colb = jnp.broadcast_to(a_cs[:, e:e + 1], (L, L))
                decay = jnp.where(tri, jnp.exp(colb - a_cs_t[e:e + 1, :]), 0.0)
                m_parts.append((cb * decay * dt_t[e:e + 1, :]).astype(BF16))
                cs_parts.append((c_g * jnp.exp(colb)).astype(BF16))
                bw_parts.append((b_t * w_t[e:e + 1, :]).astype(BF16))
            pair = g * (heads_per_group // 2) + pr
            e0 = 2 * pair
            x_bd = block_diag(xc[:, pair * LANES:(pair + 1) * LANES])
            st = state_ref[pair]
            rhs = jnp.concatenate([x_bd, block_diag(st)], axis=0)
            y_parts.append(jnp.dot(jnp.concatenate(m_parts + cs_parts, axis=1), rhs,
                                   preferred_element_type=F32))
            st_new = jnp.dot(jnp.concatenate(bw_parts, axis=1), x_bd,
                             preferred_element_type=F32)
            dec = jnp.where(lo, chunk_decay[e0:e0 + 1, :], chunk_decay[e0 + 1:e0 + 2, :])
            state_ref[pair] = st * dec + st_new

    y = jnp.concatenate(y_parts, axis=1) + dskip_ref[...] * xc
    y = y * _silu(z_ref[...])
    gw = SSD_WIDTH // SSD_GROUPS
    outs = []
    for g in range(SSD_GROUPS):
        yg = y[:, g * gw:(g + 1) * gw]
        outs.append(yg * _rms_scale(yg) * gn_ref[:, g * gw:(g + 1) * gw])
    o_ref[...] = jnp.concatenate(outs, axis=1)


def _ssd(proj, conv_w, conv_b, dt_bias, a_log, d_skip, gn, layer):
    B, S, _ = proj.shape
    L = SSD_CHUNK
    col = lambda width, off: pl.BlockSpec((None, L, width), lambda b, c: (b, c, off // width))
    par = lambda rows, width, off: pl.BlockSpec((None, rows, width), lambda b, c: (layer, 0, off // width))
    return pl.pallas_call(
        _ssd_kernel,
        grid=(B, S // L),
        in_specs=[
            col(SSD_WIDTH, OFF_Z), col(SSD_WIDTH, OFF_X), col(SSD_BC, OFF_BC), col(LANES, OFF_DT),
            par(SSD_CONV, SSD_WIDTH, 0), par(1, SSD_WIDTH, 0),
            par(SSD_CONV, SSD_BC, SSD_WIDTH), par(1, SSD_BC, SSD_WIDTH),
            par(1, LANES, 0), par(1, LANES, 0), par(1, SSD_WIDTH, 0), par(1, SSD_WIDTH, 0),
        ],
        out_specs=pl.BlockSpec((None, L, SSD_WIDTH), lambda b, c: (b, c, 0)),
        out_shape=jax.ShapeDtypeStruct((B, S, SSD_WIDTH), F32),
        scratch_shapes=[
            pltpu.VMEM((L + 2 * SUBLANES, SSD_WIDTH), F32),
            pltpu.VMEM((L + 2 * SUBLANES, SSD_BC), F32),
            pltpu.VMEM((SSD_HEADS // 2, SSD_STATE, LANES), F32),
        ],
        compiler_params=pltpu.CompilerParams(
            dimension_semantics=("parallel", "arbitrary"), vmem_limit_bytes=VMEM_LIMIT),
        name="ssd_scan",
    )(proj, proj, proj, proj, conv_w, conv_b, conv_w, conv_b, dt_bias, a_log, d_skip, gn)


def _out_proj_kernel(a_ref, y_ref, x_ref, ga_ref, gp_ref, w_ref, o_ref):
    a = a_ref[...]
    an = (a * _rms_scale(a) * ga_ref[...]).astype(BF16)
    o = jnp.dot(an, w_ref[0:ATTN_WIDTH, :], preferred_element_type=F32)
    o = o + jnp.dot(y_ref[...].astype(BF16), w_ref[ATTN_WIDTH:, :], preferred_element_type=F32)
    o_ref[...] = x_ref[...] + o * _rms_scale(o) * gp_ref[...]


def _out_proj(attn, y, x, g_attn, g_post, w, layer, *, tm=512):
    T, D = x.shape
    vec = lambda width: pl.BlockSpec((None, 1, width), lambda i: (layer, 0, 0))
    return pl.pallas_call(
        _out_proj_kernel,
        grid=(T // tm,),
        in_specs=[
            pl.BlockSpec((tm, ATTN_WIDTH), lambda i: (i, 0)),
            pl.BlockSpec((tm, SSD_WIDTH), lambda i: (i, 0)),
            pl.BlockSpec((tm, D), lambda i: (i, 0)),
            vec(ATTN_WIDTH), vec(D),
            pl.BlockSpec((None, ATTN_WIDTH + SSD_WIDTH, D), lambda i: (layer, 0, 0)),
        ],
        out_specs=pl.BlockSpec((tm, D), lambda i: (i, 0)),
        out_shape=jax.ShapeDtypeStruct((T, D), F32),
        compiler_params=pltpu.CompilerParams(
            dimension_semantics=("parallel",), vmem_limit_bytes=VMEM_LIMIT),
        name="out_proj",
    )(attn, y, x, g_attn, g_post, w)


def _mlp_kernel(x_ref, g1_ref, g2_ref, wu_ref, wd_ref, o_ref, h_ref, acc_ref):
    f = pl.program_id(1)

    @pl.when(f == 0)
    def _():
        x = x_ref[...]
        h_ref[...] = ((x * _rms_scale(x)) * g1_ref[...]).astype(BF16)
        acc_ref[...] = jnp.zeros(acc_ref.shape, F32)

    u = jnp.maximum(jnp.dot(h_ref[...], wu_ref[...], preferred_element_type=F32), 0.0)
    acc_ref[...] += jnp.dot((u * u).astype(BF16), wd_ref[...], preferred_element_type=F32)

    @pl.when(f == pl.num_programs(1) - 1)
    def _():
        acc = acc_ref[...]
        o_ref[...] = x_ref[...] + acc * _rms_scale(acc) * g2_ref[...]


def _mlp(x, g_pre, g_post, w_up, w_down, layer, *, tm=512, tf=512):
    T, D = x.shape
    F = w_up.shape[-1]
    vec = pl.BlockSpec((None, 1, D), lambda i, f: (layer, 0, 0))
    return pl.pallas_call(
        _mlp_kernel,
        grid=(T // tm, F // tf),
        in_specs=[
            pl.BlockSpec((tm, D), lambda i, f: (i, 0)),
            vec, vec,
            pl.BlockSpec((None, D, tf), lambda i, f: (layer, 0, f)),
            pl.BlockSpec((None, tf, D), lambda i, f: (layer, f, 0)),
        ],
        out_specs=pl.BlockSpec((tm, D), lambda i, f: (i, 0)),
        out_shape=jax.ShapeDtypeStruct((T, D), F32),
        scratch_shapes=[pltpu.VMEM((tm, D), BF16), pltpu.VMEM((tm, D), F32)],
        compiler_params=pltpu.CompilerParams(
            dimension_semantics=("parallel", "arbitrary"), vmem_limit_bytes=VMEM_LIMIT),
        name="mlp",
    )(x, g_pre, g_post, w_up, w_down)


def _row(p):
    return p[:, None, :]


def _pad_lanes(p):
    return jnp.pad(p, ((0, 0), (0, LANES - p.shape[-1])))


def kernel(x, positions, pre_mix_norm, post_mix_norm, pre_mlp_norm, post_mlp_norm, w_in, conv_w, conv_b, dt_bias, a_log, d_skip, attn_out_norm, ssd_out_norm, w_out, w_up, w_down):
    B, S, D = x.shape
    depth = w_in.shape[0]
    T = B * S

    w_in_b = jnp.pad(w_in, ((0, 0), (0, 0), (0, IN_PAD - IN_DIM))).astype(BF16)
    w_out_b = w_out.astype(BF16)
    w_up_b = w_up.astype(BF16)
    w_down_b = w_down.astype(BF16)
    dt_bias_p = _row(_pad_lanes(dt_bias))
    a_log_p = _row(_pad_lanes(a_log))
    d_skip_x = _row(jnp.repeat(d_skip, SSD_HEAD_DIM, axis=-1))
    conv_b_r = _row(conv_b)
    g_pre_mix, g_post_mix = _row(pre_mix_norm), _row(post_mix_norm)
    g_pre_mlp, g_post_mlp = _row(pre_mlp_norm), _row(post_mlp_norm)
    g_attn, g_ssd = _row(attn_out_norm), _row(ssd_out_norm)

    c_tab, sa_tab, sb_tab = _rope_tables(positions)

    xf = x.reshape(T, D)
    for l in range(depth):
        proj = _norm_matmul(xf, g_pre_mix, w_in_b, l).reshape(B, S, IN_PAD)
        attn = _moba_attention(proj, c_tab, sa_tab, sb_tab)
        y = _ssd(proj, conv_w, conv_b_r, dt_bias_p, a_log_p, d_skip_x, g_ssd, l)
        xf = _out_proj(attn.reshape(T, ATTN_WIDTH), y.reshape(T, SSD_WIDTH), xf,
                       g_attn, g_post_mix, w_out_b, l)
        xf = _mlp(xf, g_pre_mlp, g_post_mlp, w_up_b, w_down_b, l)
    return xf.reshape(B, S, D)
```

```python
import functools
import math

import jax
import jax.numpy as jnp
from jax import lax
from jax.experimental import pallas as pl
from jax.experimental.pallas import tpu as pltpu

F32 = jnp.float32
BF16 = jnp.bfloat16

D_MODEL = 2048
ATTN_WIDTH = 1024
HEAD_DIM = 128
ATTN_HEADS = ATTN_WIDTH // HEAD_DIM
ROPE_DIM = HEAD_DIM // 4
ROPE_HALF = ROPE_DIM // 2
ROPE_THETA = 500000.0
MOBA_BLOCK = 256
MOBA_TOPK = 3

SSD_WIDTH = 1024
SSD_HEAD_DIM = 64
SSD_HEADS = SSD_WIDTH // SSD_HEAD_DIM
SSD_GROUPS = 2
SSD_STATE = 128
SSD_CONV = 4
SSD_CHUNK = 128
SSD_BC = 2 * SSD_GROUPS * SSD_STATE
CONV_DIM = SSD_WIDTH + SSD_BC
IN_DIM = 3 * ATTN_WIDTH + SSD_WIDTH + CONV_DIM + SSD_HEADS
LANES = 128
SUBLANES = 8
IN_PAD = -(-IN_DIM // LANES) * LANES
D_FF = 4 * D_MODEL
NORM_EPS = 1e-6

OFF_Z = 3 * ATTN_WIDTH
OFF_X = OFF_Z + SSD_WIDTH
OFF_BC = OFF_X + SSD_WIDTH
OFF_DT = OFF_BC + SSD_BC

VMEM_LIMIT = 56 * 1024 * 1024

NT_DIMS = (((1,), (1,)), ((), ()))


def _rms_scale(v):
    return lax.rsqrt(jnp.mean(v * v, axis=-1, keepdims=True) + NORM_EPS)


def _silu(v):
    return v * jax.nn.sigmoid(v)


def _rope_table_kernel(pos_ref, c_ref, sa_ref, sb_ref):
    pos = pos_ref[...].astype(F32)
    lane = lax.broadcasted_iota(jnp.int32, (1, LANES), 1)
    k = (lane & (ROPE_HALF - 1)).astype(F32)
    inv_freq = jnp.exp(k * (-math.log(ROPE_THETA) / ROPE_HALF))
    ang = pos * inv_freq
    cos = jnp.cos(ang)
    sin = jnp.sin(ang)
    in_rope = lane < ROPE_DIM
    c_ref[...] = jnp.where(in_rope, cos, 1.0)
    sa_ref[...] = jnp.where(lane < ROPE_HALF, -sin, 0.0)
    sb_ref[...] = jnp.where(jnp.logical_and(lane >= ROPE_HALF, in_rope), sin, 0.0)


def _rope_tables(positions):
    B, S = positions.shape
    spec = pl.BlockSpec((None, S, LANES), lambda b: (b, 0, 0))
    shape = jax.ShapeDtypeStruct((B, S, LANES), F32)
    return pl.pallas_call(
        _rope_table_kernel,
        grid=(B,),
        in_specs=[pl.BlockSpec((None, S, 1), lambda b: (b, 0, 0))],
        out_specs=[spec, spec, spec],
        out_shape=[shape, shape, shape],
        name="rope_tables",
    )(positions.reshape(B, S, 1))


def _norm_matmul_kernel(x_ref, g_ref, w_ref, o_ref, h_ref):
    @pl.when(pl.program_id(1) == 0)
    def _():
        x = x_ref[...]
        h_ref[...] = ((x * _rms_scale(x)) * g_ref[...]).astype(BF16)

    o_ref[...] = jnp.dot(h_ref[...], w_ref[...], preferred_element_type=F32)


def _norm_matmul(x, gain, w, layer, *, tm=512, tn=1920):
    T, D = x.shape
    N = w.shape[-1]
    return pl.pallas_call(
        _norm_matmul_kernel,
        grid=(T // tm, N // tn),
        in_specs=[
            pl.BlockSpec((tm, D), lambda i, j: (i, 0)),
            pl.BlockSpec((None, 1, D), lambda i, j: (layer, 0, 0)),
            pl.BlockSpec((None, D, tn), lambda i, j: (layer, 0, j)),
        ],
        out_specs=pl.BlockSpec((tm, tn), lambda i, j: (i, j)),
        out_shape=jax.ShapeDtypeStruct((T, N), F32),
        scratch_shapes=[pltpu.VMEM((tm, D), BF16)],
        compiler_params=pltpu.CompilerParams(
            dimension_semantics=("parallel", "arbitrary"), vmem_limit_bytes=VMEM_LIMIT),
        name="norm_in_proj",
    )(x, gain, w)


def _split_bf16(t):
    hi = t.astype(BF16)
    return hi, (t - hi.astype(F32)).astype(BF16)


def _moba_kernel(q_ref, k_ref, v_ref, c_ref, sa_ref, sb_ref, o_ref,
                 kr_ref, vt_ref, km_ref, *, n_blocks):
    i = pl.program_id(2)
    blk = MOBA_BLOCK

    def rope(t, rows):
        return (t * c_ref[rows, :]
                + pltpu.roll(t, LANES - ROPE_HALF, 1) * sa_ref[rows, :]
                + pltpu.roll(t, ROPE_HALF, 1) * sb_ref[rows, :])

    @pl.when(i == 0)
    def _():
        for b in range(n_blocks):
            rows = pl.ds(b * blk, blk)
            kb = rope(k_ref[rows, :], rows)
            kr_ref[rows, :] = kb.astype(BF16)
            km_ref[b:b + 1, :] = jnp.mean(kb, axis=0, keepdims=True)
            vt_ref[:, b * blk:(b + 1) * blk] = v_ref[rows, :].T.astype(BF16)

    q = rope(q_ref[...], pl.ds(pl.multiple_of(i * blk, blk), blk)) * (HEAD_DIM ** -0.5)

    q_hi, q_lo = _split_bf16(q)
    km_hi, km_lo = _split_bf16(km_ref[...])
    gate = lax.dot_general(jnp.concatenate([km_hi, km_lo, km_hi], axis=1),
                           jnp.concatenate([q_hi, q_hi, q_lo], axis=1),
                           NT_DIMS, preferred_element_type=F32)
    jidx = lax.broadcasted_iota(jnp.int32, gate.shape, 0)
    past = jidx < i
    gate = jnp.where(past, gate, -jnp.inf)
    rank = jnp.zeros(gate.shape, jnp.int32)
    for m in range(n_blocks):
        gm = gate[m:m + 1, :]
        beats = jnp.logical_or(gm > gate, jnp.logical_and(gm == gate, m < jidx))
        rank = rank + beats.astype(jnp.int32)
    sel = jnp.logical_and(past, rank < MOBA_TOPK).astype(F32)

    key_i = lax.broadcasted_iota(jnp.int32, (blk, blk), 0)
    qry_i = lax.broadcasted_iota(jnp.int32, (blk, blk), 1)
    causal = key_i <= qry_i

    def attend(n):
        nk = (n + 1) * blk
        s = lax.dot_general(kr_ref[0:nk, :], q_hi, NT_DIMS, preferred_element_type=F32)
        parts = [jnp.where(sel[j:j + 1, :] > 0.0, s[j * blk:(j + 1) * blk, :], -jnp.inf) for j in range(n)]
        parts.append(jnp.where(causal, s[n * blk:nk, :], -jnp.inf))
        m_max = functools.reduce(jnp.maximum, [jnp.max(t, axis=0, keepdims=True) for t in parts])
        probs = [jnp.exp(t - m_max) for t in parts]
        denom = functools.reduce(jnp.add, [jnp.sum(t, axis=0, keepdims=True) for t in probs])
        acc = jnp.dot(vt_ref[:, 0:nk], jnp.concatenate(probs, axis=0).astype(BF16),
                      preferred_element_type=F32)
        o_ref[...] = (acc * (1.0 / denom)).T

    for n in range(n_blocks):
        pl.when(i == n)(functools.partial(attend, n))


def _moba_attention(proj, c_tab, sa_tab, sb_tab):
    B, S, _ = proj.shape
    nb = S // MOBA_BLOCK
    H = ATTN_HEADS
    full = lambda off: pl.BlockSpec((None, S, HEAD_DIM), lambda b, h, i: (b, 0, off + h))
    tab = pl.BlockSpec((None, S, LANES), lambda b, h, i: (b, 0, 0))
    return pl.pallas_call(
        functools.partial(_moba_kernel, n_blocks=nb),
        grid=(B, H, nb),
        in_specs=[
            pl.BlockSpec((None, MOBA_BLOCK, HEAD_DIM), lambda b, h, i: (b, i, h)),
            full(H), full(2 * H), tab, tab, tab,
        ],
        out_specs=pl.BlockSpec((None, MOBA_BLOCK, HEAD_DIM), lambda b, h, i: (b, i, h)),
        out_shape=jax.ShapeDtypeStruct((B, S, ATTN_WIDTH), F32),
        scratch_shapes=[
            pltpu.VMEM((S, HEAD_DIM), BF16),
            pltpu.VMEM((HEAD_DIM, S), BF16),
            pltpu.VMEM((nb, HEAD_DIM), F32),
        ],
        compiler_params=pltpu.CompilerParams(
            dimension_semantics=("parallel", "parallel", "arbitrary"), vmem_limit_bytes=VMEM_LIMIT),
        name="moba_attention",
    )(proj, proj, proj, c_tab, sa_tab, sb_tab)


def _ssd_kernel(z_ref, xs_ref, bc_ref, dt_ref, wx_ref, bx_ref, wbc_ref, bbc_ref,
                dtb_ref, alog_ref, dskip_ref, gn_ref, o_ref,
                extx_ref, extbc_ref, state_ref):
    L = SSD_CHUNK
    N = SSD_STATE
    halo = SUBLANES

    @pl.when(pl.program_id(1) == 0)
    def _():
        extx_ref[0:halo, :] = jnp.zeros((halo, SSD_WIDTH), F32)
        extbc_ref[0:halo, :] = jnp.zeros((halo, SSD_BC), F32)
        state_ref[...] = jnp.zeros(state_ref.shape, F32)

    def conv_silu(ext_ref, cur_ref, w_ref, b_ref):
        ext_ref[halo:halo + L, :] = cur_ref[...]
        acc = b_ref[...] + ext_ref[pl.ds(halo, L), :] * w_ref[SSD_CONV - 1:SSD_CONV, :]
        for k in range(SSD_CONV - 1):
            acc = acc + ext_ref[pl.ds(halo - (SSD_CONV - 1) + k, L), :] * w_ref[k:k + 1, :]
        ext_ref[0:halo, :] = ext_ref[L:L + halo, :]
        return _silu(acc)

    xc = conv_silu(extx_ref, xs_ref, wx_ref, bx_ref)
    bcc = conv_silu(extbc_ref, bc_ref, wbc_ref, bbc_ref)

    dt_in = dt_ref[...] + dtb_ref[...]
    dt = jnp.maximum(dt_in, 0.0) + jnp.log(1.0 + jnp.exp(-jnp.abs(dt_in)))
    a_dt = dt * (-jnp.exp(alog_ref[...]))
    row = lax.broadcasted_iota(jnp.int32, (L, L), 0)
    col = lax.broadcasted_iota(jnp.int32, (L, L), 1)
    tri = row >= col
    a_cs = jnp.dot(tri.astype(F32), a_dt, precision=lax.Precision.HIGHEST,
                   preferred_element_type=F32)
    a_cs_t = a_cs.T[0:SSD_HEADS, :]
    dt_t = dt.T[0:SSD_HEADS, :]
    a_last = a_cs_t[:, L - 1:L]
    w_t = dt_t * jnp.exp(a_last - a_cs_t)
    chunk_decay = jnp.broadcast_to(jnp.exp(a_last), (SSD_HEADS, LANES))

    lane = lax.broadcasted_iota(jnp.int32, (1, LANES), 1)
    lo = lane < SSD_HEAD_DIM

    def block_diag(v):
        return jnp.concatenate([jnp.where(lo, v, 0.0), jnp.where(lo, 0.0, v)], axis=0).astype(BF16)

    heads_per_group = SSD_HEADS // SSD_GROUPS
    y_parts = []
    for g in range(SSD_GROUPS):
        b_g = bcc[:, g * N:(g + 1) * N]
        c_g = bcc[:, (SSD_GROUPS + g) * N:(SSD_GROUPS + g + 1) * N]
        cb = lax.dot_general(c_g.astype(BF16), b_g.astype(BF16), NT_DIMS,
                             preferred_element_type=F32)
        b_t = b_g.T
        for pr in range(heads_per_group // 2):
            m_parts, cs_parts, bw_parts = [], [], []
            for e in (g * heads_per_group + 2 * pr, g * heads_per_group + 2 * pr + 1):
                colb = jnp.broadcast_to(a_cs[:, e:e + 1], (L, L))
                decay = jnp.where(tri, jnp.exp(colb - a_cs_t[e:e + 1, :]), 0.0)
                m_parts.append((cb * decay * dt_t[e:e + 1, :]).astype(BF16))
                cs_parts.append((c_g * jnp.exp(colb)).astype(BF16))
                bw_parts.append((b_t * w_t[e:e + 1, :]).astype(BF16))
            pair = g * (heads_per_group // 2) + pr
            e0 = 2 * pair
            x_bd = block_diag(xc[:, pair * LANES:(pair + 1) * LANES])
            st = state_ref[pair]
            rhs = jnp.concatenate([x_bd, block_diag(st)], axis=0)
            y_parts.append(jnp.dot(jnp.concatenate(m_parts + cs_parts, axis=1), rhs,
                                   preferred_element_type=F32))
            st_new = jnp.dot(jnp.concatenate(bw_parts, axis=1), x_bd,
                             preferred_element_type=F32)
            dec = jnp.where(lo, chunk_decay[e0:e0 + 1, :], chunk_decay[e0 + 1:e0 + 2, :])
            state_ref[pair] = st * dec + st_new

    y = jnp.concatenate(y_parts, axis=1) + dskip_ref[...] * xc
    y = y * _silu(z_ref[...])
    gw = SSD_WIDTH // SSD_GROUPS
    outs = []
    for g in range(SSD_GROUPS):
        yg = y[:, g * gw:(g + 1) * gw]
        outs.append(yg * _rms_scale(yg) * gn_ref[:, g * gw:(g + 1) * gw])
    o_ref[...] = jnp.concatenate(outs, axis=1)


def _ssd(proj, conv_w, conv_b, dt_bias, a_log, d_skip, gn, layer):
    B, S, _ = proj.shape
    L = SSD_CHUNK
    col = lambda width, off: pl.BlockSpec((None, L, width), lambda b, c: (b, c, off // width))
    par = lambda rows, width, off: pl.BlockSpec((None, rows, width), lambda b, c: (layer, 0, off // width))
    return pl.pallas_call(
        _ssd_kernel,
        grid=(B, S // L),
        in_specs=[
            col(SSD_WIDTH, OFF_Z), col(SSD_WIDTH, OFF_X), col(SSD_BC, OFF_BC), col(LANES, OFF_DT),
            par(SSD_CONV, SSD_WIDTH, 0), par(1, SSD_WIDTH, 0),
            par(SSD_CONV, SSD_BC, SSD_WIDTH), par(1, SSD_BC, SSD_WIDTH),
            par(1, LANES, 0), par(1, LANES, 0), par(1, SSD_WIDTH, 0), par(1, SSD_WIDTH, 0),
        ],
        out_specs=pl.BlockSpec((None, L, SSD_WIDTH), lambda b, c: (b, c, 0)),
        out_shape=jax.ShapeDtypeStruct((B, S, SSD_WIDTH), F32),
        scratch_shapes=[
            pltpu.VMEM((L + 2 * SUBLANES, SSD_WIDTH), F32),
            pltpu.VMEM((L + 2 * SUBLANES, SSD_BC), F32),
            pltpu.VMEM((SSD_HEADS // 2, SSD_STATE, LANES), F32),
        ],
        compiler_params=pltpu.CompilerParams(
            dimension_semantics=("parallel", "arbitrary"), vmem_limit_bytes=VMEM_LIMIT),
        name="ssd_scan",
    )(proj, proj, proj, proj, conv_w, conv_b, conv_w, conv_b, dt_bias, a_log, d_skip, gn)


def _out_proj_kernel(a_ref, y_ref, x_ref, ga_ref, gp_ref, w_ref, o_ref):
    a = a_ref[...]
    an = (a * _rms_scale(a) * ga_ref[...]).astype(BF16)
    o = jnp.dot(an, w_ref[0:ATTN_WIDTH, :], preferred_element_type=F32)
    o = o + jnp.dot(y_ref[...].astype(BF16), w_ref[ATTN_WIDTH:, :], preferred_element_type=F32)
    o_ref[...] = x_ref[...] + o * _rms_scale(o) * gp_ref[...]


def _out_proj(attn, y, x, g_attn, g_post, w, layer, *, tm=512):
    T, D = x.shape
    vec = lambda width: pl.BlockSpec((None, 1, width), lambda i: (layer, 0, 0))
    return pl.pallas_call(
        _out_proj_kernel,
        grid=(T // tm,),
        in_specs=[
            pl.BlockSpec((tm, ATTN_WIDTH), lambda i: (i, 0)),
            pl.BlockSpec((tm, SSD_WIDTH), lambda i: (i, 0)),
            pl.BlockSpec((tm, D), lambda i: (i, 0)),
            vec(ATTN_WIDTH), vec(D),
            pl.BlockSpec((None, ATTN_WIDTH + SSD_WIDTH, D), lambda i: (layer, 0, 0)),
        ],
        out_specs=pl.BlockSpec((tm, D), lambda i: (i, 0)),
        out_shape=jax.ShapeDtypeStruct((T, D), F32),
        compiler_params=pltpu.CompilerParams(
            dimension_semantics=("parallel",), vmem_limit_bytes=VMEM_LIMIT),
        name="out_proj",
    )(attn, y, x, g_attn, g_post, w)


def _mlp_kernel(x_ref, g1_ref, g2_ref, wu_ref, wd_ref, o_ref, h_ref, acc_ref):
    f = pl.program_id(1)

    @pl.when(f == 0)
    def _():
        x = x_ref[...]
        h_ref[...] = ((x * _rms_scale(x)) * g1_ref[...]).astype(BF16)
        acc_ref[...] = jnp.zeros(acc_ref.shape, F32)

    u = jnp.maximum(jnp.dot(h_ref[...], wu_ref[...], preferred_element_type=F32), 0.0)
    acc_ref[...] += jnp.dot((u * u).astype(BF16), wd_ref[...], preferred_element_type=F32)

    @pl.when(f == pl.num_programs(1) - 1)
    def _():
        acc = acc_ref[...]
        o_ref[...] = x_ref[...] + acc * _rms_scale(acc) * g2_ref[...]


def _mlp(x, g_pre, g_post, w_up, w_down, layer, *, tm=512, tf=512):
    T, D = x.shape
    F = w_up.shape[-1]
    vec = pl.BlockSpec((None, 1, D), lambda i, f: (layer, 0, 0))
    return pl.pallas_call(
        _mlp_kernel,
        grid=(T // tm, F // tf),
        in_specs=[
            pl.BlockSpec((tm, D), lambda i, f: (i, 0)),
            vec, vec,
            pl.BlockSpec((None, D, tf), lambda i, f: (layer, 0, f)),
            pl.BlockSpec((None, tf, D), lambda i, f: (layer, f, 0)),
        ],
        out_specs=pl.BlockSpec((tm, D), lambda i, f: (i, 0)),
        out_shape=jax.ShapeDtypeStruct((T, D), F32),
        scratch_shapes=[pltpu.VMEM((tm, D), BF16), pltpu.VMEM((tm, D), F32)],
        compiler_params=pltpu.CompilerParams(
            dimension_semantics=("parallel", "arbitrary"), vmem_limit_bytes=VMEM_LIMIT),
        name="mlp",
    )(x, g_pre, g_post, w_up, w_down)


def _row(p):
    return p[:, None, :]


def _pad_lanes(p):
    return jnp.pad(p, ((0, 0), (0, LANES - p.shape[-1])))


def kernel(x, positions, pre_mix_norm, post_mix_norm, pre_mlp_norm, post_mlp_norm, w_in, conv_w, conv_b, dt_bias, a_log, d_skip, attn_out_norm, ssd_out_norm, w_out, w_up, w_down):
    B, S, D = x.shape
    depth = w_in.shape[0]
    T = B * S

    w_in_b = jnp.pad(w_in, ((0, 0), (0, 0), (0, IN_PAD - IN_DIM))).astype(BF16)
    w_out_b = w_out.astype(BF16)
    w_up_b = w_up.astype(BF16)
    w_down_b = w_down.astype(BF16)
    dt_bias_p = _row(_pad_lanes(dt_bias))
    a_log_p = _row(_pad_lanes(a_log))
    d_skip_x = _row(jnp.repeat(d_skip, SSD_HEAD_DIM, axis=-1))
    conv_b_r = _row(conv_b)
    g_pre_mix, g_post_mix = _row(pre_mix_norm), _row(post_mix_norm)
    g_pre_mlp, g_post_mlp = _row(pre_mlp_norm), _row(post_mlp_norm)
    g_attn, g_ssd = _row(attn_out_norm), _row(ssd_out_norm)

    c_tab, sa_tab, sb_tab = _rope_tables(positions)

    xf = x.reshape(T, D)
    for l in range(depth):
        proj = _norm_matmul(xf, g_pre_mix, w_in_b, l).reshape(B, S, IN_PAD)
        attn = _moba_attention(proj, c_tab, sa_tab, sb_tab)
        y = _ssd(proj, conv_w, conv_b_r, dt_bias_p, a_log_p, d_skip_x, g_ssd, l)
        xf = _out_proj(attn.reshape(T, ATTN_WIDTH), y.reshape(T, SSD_WIDTH), xf,
                       g_attn, g_post_mix, w_out_b, l)
        xf = _mlp(xf, g_pre_mlp, g_post_mlp, w_up_b, w_down_b, l)
    return xf.reshape(B, S, D)
```

```python
import functools
import math

import jax
import jax.numpy as jnp
from jax import lax
from jax.experimental import pallas as pl
from jax.experimental.pallas import tpu as pltpu

F32 = jnp.float32
BF16 = jnp.bfloat16

D_MODEL = 2048
ATTN_WIDTH = 1024
HEAD_DIM = 128
ATTN_HEADS = ATTN_WIDTH // HEAD_DIM
ROPE_DIM = HEAD_DIM // 4
ROPE_HALF = ROPE_DIM // 2
ROPE_THETA = 500000.0
MOBA_BLOCK = 256
MOBA_TOPK = 3
MOBA_HEADS_PER_STEP = 2

SSD_WIDTH = 1024
SSD_HEAD_DIM = 64
SSD_HEADS = SSD_WIDTH // SSD_HEAD_DIM
SSD_GROUPS = 2
SSD_STATE = 128
SSD_CONV = 4
SSD_CHUNK = 128
SSD_BC = 2 * SSD_GROUPS * SSD_STATE
CONV_DIM = SSD_WIDTH + SSD_BC
IN_DIM = 3 * ATTN_WIDTH + SSD_WIDTH + CONV_DIM + SSD_HEADS
LANES = 128
SUBLANES = 8
IN_PAD = -(-IN_DIM // LANES) * LANES
D_FF = 4 * D_MODEL
NORM_EPS = 1e-6

OFF_Z = 3 * ATTN_WIDTH
OFF_X = OFF_Z + SSD_WIDTH
OFF_BC = OFF_X + SSD_WIDTH
OFF_DT = OFF_BC + SSD_BC

VMEM_LIMIT = 56 * 1024 * 1024

NT_DIMS = (((1,), (1,)), ((), ()))
LOG2E = math.log2(math.e)


def _rms_scale(v):
    return lax.rsqrt(jnp.mean(v * v, axis=-1, keepdims=True) + NORM_EPS)


def _silu(v):
    return v * jax.nn.sigmoid(v)


def _rope_table_kernel(pos_ref, c_ref, sa_ref, sb_ref):
    pos = pos_ref[...].astype(F32)
    lane = lax.broadcasted_iota(jnp.int32, (1, LANES), 1)
    k = (lane & (ROPE_HALF - 1)).astype(F32)
    inv_freq = jnp.exp(k * (-math.log(ROPE_THETA) / ROPE_HALF))
    ang = pos * inv_freq
    cos = jnp.cos(ang)
    sin = jnp.sin(ang)
    in_rope = lane < ROPE_DIM
    c_ref[...] = jnp.where(in_rope, cos, 1.0)
    sa_ref[...] = jnp.where(lane < ROPE_HALF, -sin, 0.0)
    sb_ref[...] = jnp.where(jnp.logical_and(lane >= ROPE_HALF, in_rope), sin, 0.0)


def _rope_tables(positions):
    B, S = positions.shape
    spec = pl.BlockSpec((None, S, LANES), lambda b: (b, 0, 0))
    shape = jax.ShapeDtypeStruct((B, S, LANES), F32)
    return pl.pallas_call(
        _rope_table_kernel,
        grid=(B,),
        in_specs=[pl.BlockSpec((None, S, 1), lambda b: (b, 0, 0))],
        out_specs=[spec, spec, spec],
        out_shape=[shape, shape, shape],
        name="rope_tables",
    )(positions.reshape(B, S, 1))


def _norm_matmul_kernel(x_ref, g_ref, w_ref, o_ref, h_ref):
    @pl.when(pl.program_id(1) == 0)
    def _():
        x = x_ref[...]
        h_ref[...] = ((x * _rms_scale(x)) * g_ref[...]).astype(BF16)

    o_ref[...] = jnp.dot(h_ref[...], w_ref[...], preferred_element_type=F32)


def _norm_matmul(x, gain, w, layer, *, tm=1024, tn=1920):
    T, D = x.shape
    N = w.shape[-1]
    return pl.pallas_call(
        _norm_matmul_kernel,
        grid=(T // tm, N // tn),
        in_specs=[
            pl.BlockSpec((tm, D), lambda i, j: (i, 0)),
            pl.BlockSpec((None, 1, D), lambda i, j: (layer, 0, 0)),
            pl.BlockSpec((None, D, tn), lambda i, j: (layer, 0, j)),
        ],
        out_specs=pl.BlockSpec((tm, tn), lambda i, j: (i, j)),
        out_shape=jax.ShapeDtypeStruct((T, N), F32),
        scratch_shapes=[pltpu.VMEM((tm, D), BF16)],
        compiler_params=pltpu.CompilerParams(
            dimension_semantics=("parallel", "arbitrary"), vmem_limit_bytes=VMEM_LIMIT),
        name="norm_in_proj",
    )(x, gain, w)


def _split_bf16(t):
    hi = t.astype(BF16)
    return hi, (t - hi.astype(F32)).astype(BF16)


def _moba_kernel(q_ref, k_ref, v_ref, c_ref, sa_ref, sb_ref, o_ref,
                 kr_ref, vt_ref, km_ref, *, n_blocks):
    i = pl.program_id(2)
    blk = MOBA_BLOCK

    def rope(t, rows):
        return (t * c_ref[rows, :]
                + pltpu.roll(t, LANES - ROPE_HALF, 1) * sa_ref[rows, :]
                + pltpu.roll(t, ROPE_HALF, 1) * sb_ref[rows, :])

    heads = range(kr_ref.shape[0])
    cols = lambda hh: slice(hh * HEAD_DIM, (hh + 1) * HEAD_DIM)

    @pl.when(i == 0)
    def _():
        for hh in heads:
            for b in range(n_blocks):
                rows = pl.ds(b * blk, blk)
                kb = rope(k_ref[rows, cols(hh)], rows)
                kr_ref[hh, rows, :] = kb.astype(BF16)
                km_ref[hh, b:b + 1, :] = jnp.mean(kb, axis=0, keepdims=True)
                vt_ref[hh, :, b * blk:(b + 1) * blk] = v_ref[rows, cols(hh)].T.astype(BF16)

    q_rows = pl.ds(pl.multiple_of(i * blk, blk), blk)
    q_atts, biases = [], []
    for hh in heads:
        q = rope(q_ref[:, cols(hh)], q_rows) * (HEAD_DIM ** -0.5)
        q_hi, q_lo = _split_bf16(q)
        km_hi, km_lo = _split_bf16(km_ref[hh])
        gate = lax.dot_general(jnp.concatenate([km_hi, km_lo, km_hi], axis=1),
                               jnp.concatenate([q_hi, q_hi, q_lo], axis=1),
                               NT_DIMS, preferred_element_type=F32)
        jidx = lax.broadcasted_iota(jnp.int32, gate.shape, 0)
        past = jidx < i
        gate = jnp.where(past, gate, -jnp.inf)
        rank = jnp.zeros(gate.shape, jnp.int32)
        for m in range(n_blocks):
            gm = gate[m:m + 1, :]
            beats = jnp.logical_or(gm > gate, jnp.logical_and(gm == gate, m < jidx))
            rank = rank + beats.astype(jnp.int32)
        q_atts.append((q * LOG2E).astype(BF16))
        selected = jnp.logical_and(past, rank < MOBA_TOPK)
        biases.append(jnp.where(selected, 0.0, -jnp.inf))

    key_i = lax.broadcasted_iota(jnp.int32, (blk, blk), 0)
    qry_i = lax.broadcasted_iota(jnp.int32, (blk, blk), 1)
    causal_bias = jnp.where(key_i <= qry_i, 0.0, -jnp.inf)

    def attend(n):
        nk = (n + 1) * blk
        for hh in heads:
            bias = biases[hh]
            s = lax.dot_general(kr_ref[hh, 0:nk, :], q_atts[hh], NT_DIMS,
                                preferred_element_type=F32)
            parts = [s[j * blk:(j + 1) * blk, :] + bias[j:j + 1, :] for j in range(n)]
            parts.append(s[n * blk:nk, :] + causal_bias)
            m_max = functools.reduce(jnp.maximum, [jnp.max(t, axis=0, keepdims=True) for t in parts])
            probs = [jnp.exp2(t - m_max) for t in parts]
            denom = functools.reduce(jnp.add, [jnp.sum(t, axis=0, keepdims=True) for t in probs])
            acc = jnp.dot(vt_ref[hh, :, 0:nk], jnp.concatenate(probs, axis=0).astype(BF16),
                          preferred_element_type=F32)
            o_ref[:, cols(hh)] = (acc * (1.0 / denom)).T

    for n in range(n_blocks):
        pl.when(i == n)(functools.partial(attend, n))


def _moba_attention(proj, c_tab, sa_tab, sb_tab):
    B, S, _ = proj.shape
    nb = S // MOBA_BLOCK
    hp = MOBA_HEADS_PER_STEP
    G = ATTN_HEADS // hp
    width = hp * HEAD_DIM
    full = lambda off: pl.BlockSpec((None, S, width), lambda b, h, i: (b, 0, off + h))
    tab = pl.BlockSpec((None, S, LANES), lambda b, h, i: (b, 0, 0))
    return pl.pallas_call(
        functools.partial(_moba_kernel, n_blocks=nb),
        grid=(B, G, nb),
        in_specs=[
            pl.BlockSpec((None, MOBA_BLOCK, width), lambda b, h, i: (b, i, h)),
            full(G), full(2 * G), tab, tab, tab,
        ],
        out_specs=pl.BlockSpec((None, MOBA_BLOCK, width), lambda b, h, i: (b, i, h)),
        out_shape=jax.ShapeDtypeStruct((B, S, ATTN_WIDTH), F32),
        scratch_shapes=[
            pltpu.VMEM((hp, S, HEAD_DIM), BF16),
            pltpu.VMEM((hp, HEAD_DIM, S), BF16),
            pltpu.VMEM((hp, nb, HEAD_DIM), F32),
        ],
        compiler_params=pltpu.CompilerParams(
            dimension_semantics=("parallel", "parallel", "arbitrary"), vmem_limit_bytes=VMEM_LIMIT),
        name="moba_attention",
    )(proj, proj, proj, c_tab, sa_tab, sb_tab)


def _ssd_kernel(z_ref, xs_ref, bc_ref, dt_ref, wx_ref, bx_ref, wbc_ref, bbc_ref,
                dtb_ref, alog_ref, dskip_ref, gn_ref, o_ref,
                extx_ref, extbc_ref, state_ref):
    L = SSD_CHUNK
    N = SSD_STATE
    halo = SUBLANES

    @pl.when(pl.program_id(1) == 0)
    def _():
        extx_ref[0:halo, :] = jnp.zeros((halo, SSD_WIDTH), F32)
        extbc_ref[0:halo, :] = jnp.zeros((halo, SSD_BC), F32)
        state_ref[...] = jnp.zeros(state_ref.shape, F32)

    def conv_silu(ext_ref, cur_ref, w_ref, b_ref):
        ext_ref[halo:halo + L, :] = cur_ref[...]
        acc = b_ref[...] + ext_ref[pl.ds(halo, L), :] * w_ref[SSD_CONV - 1:SSD_CONV, :]
        for k in range(SSD_CONV - 1):
            acc = acc + ext_ref[pl.ds(halo - (SSD_CONV - 1) + k, L), :] * w_ref[k:k + 1, :]
        ext_ref[0:halo, :] = ext_ref[L:L + halo, :]
        return _silu(acc)

    xc = conv_silu(extx_ref, xs_ref, wx_ref, bx_ref)
    bcc = conv_silu(extbc_ref, bc_ref, wbc_ref, bbc_ref)

    dt_in = dt_ref[...] + dtb_ref[...]
    dt = jnp.maximum(dt_in, 0.0) + jnp.log(1.0 + jnp.exp(-jnp.abs(dt_in)))
    a_dt = dt * (-jnp.exp(alog_ref[...]))
    row = lax.broadcasted_iota(jnp.int32, (L, L), 0)
    col = lax.broadcasted_iota(jnp.int32, (L, L), 1)
    tri = row >= col
    a_cs = jnp.dot(tri.astype(F32), a_dt, precision=lax.Precision.HIGHEST,
                   preferred_element_type=F32)
    a_cs_t = a_cs.T[0:SSD_HEADS, :]
    dt_t = dt.T[0:SSD_HEADS, :]
    a_last = a_cs_t[:, L - 1:L]
    w_t = dt_t * jnp.exp(a_last - a_cs_t)
    chunk_decay = jnp.broadcast_to(jnp.exp(a_last), (SSD_HEADS, LANES))

    lane = lax.broadcasted_iota(jnp.int32, (1, LANES), 1)
    lo = lane < SSD_HEAD_DIM

    def block_diag(v):
        return jnp.concatenate([jnp.where(lo, v, 0.0), jnp.where(lo, 0.0, v)], axis=0).astype(BF16)

    heads_per_group = SSD_HEADS // SSD_GROUPS
    y_parts = []
    for g in range(SSD_GROUPS):
        b_g = bcc[:, g * N:(g + 1) * N]
        c_g = bcc[:, (SSD_GROUPS + g) * N:(SSD_GROUPS + g + 1) * N]
        cb = lax.dot_general(c_g.astype(BF16), b_g.astype(BF16), NT_DIMS,
                             preferred_element_type=F32)
        b_t = b_g.T
        for pr in range(heads_per_group // 2):
            m_parts, cs_parts, bw_parts = [], [], []
            for e in (g * heads_per_group + 2 * pr, g * heads_per_group + 2 * pr + 1):
                colb = jnp.broadcast_to(a_cs[:, e:e + 1], (L, L))
                decay = jnp.where(tri, jnp.exp(colb - a_cs_t[e:e + 1, :]), 0.0)
                m_parts.append((cb * decay * dt_t[e:e + 1, :]).astype(BF16))
                cs_parts.append((c_g * jnp.exp(colb)).astype(BF16))
                bw_parts.append((b_t * w_t[e:e + 1, :]).astype(BF16))
            pair = g * (heads_per_group // 2) + pr
            e0 = 2 * pair
            x_bd = block_diag(xc[:, pair * LANES:(pair + 1) * LANES])
            st = state_ref[pair]
            rhs = jnp.concatenate([x_bd, block_diag(st)], axis=0)
            y_parts.append(jnp.dot(jnp.concatenate(m_parts + cs_parts, axis=1), rhs,
                                   preferred_element_type=F32))
            st_new = jnp.dot(jnp.concatenate(bw_parts, axis=1), x_bd,
                             preferred_element_type=F32)
            dec = jnp.where(lo, chunk_decay[e0:e0 + 1, :], chunk_decay[e0 + 1:e0 + 2, :])
            state_ref[pair] = st * dec + st_new

    y = jnp.concatenate(y_parts, axis=1) + dskip_ref[...] * xc
    y = y * _silu(z_ref[...])
    gw = SSD_WIDTH // SSD_GROUPS
    outs = []
    for g in range(SSD_GROUPS):
        yg = y[:, g * gw:(g + 1) * gw]
        outs.append(yg * _rms_scale(yg) * gn_ref[:, g * gw:(g + 1) * gw])
    o_ref[...] = jnp.concatenate(outs, axis=1)


def _ssd(proj, conv_w, conv_b, dt_bias, a_log, d_skip, gn, layer):
    B, S, _ = proj.shape
    L = SSD_CHUNK
    col = lambda width, off: pl.BlockSpec((None, L, width), lambda b, c: (b, c, off // width))
    par = lambda rows, width, off: pl.BlockSpec((None, rows, width), lambda b, c: (layer, 0, off // width))
    return pl.pallas_call(
        _ssd_kernel,
        grid=(B, S // L),
        in_specs=[
            col(SSD_WIDTH, OFF_Z), col(SSD_WIDTH, OFF_X), col(SSD_BC, OFF_BC), col(LANES, OFF_DT),
            par(SSD_CONV, SSD_WIDTH, 0), par(1, SSD_WIDTH, 0),
            par(SSD_CONV, SSD_BC, SSD_WIDTH), par(1, SSD_BC, SSD_WIDTH),
            par(1, LANES, 0), par(1, LANES, 0), par(1, SSD_WIDTH, 0), par(1, SSD_WIDTH, 0),
        ],
        out_specs=pl.BlockSpec((None, L, SSD_WIDTH), lambda b, c: (b, c, 0)),
        out_shape=jax.ShapeDtypeStruct((B, S, SSD_WIDTH), F32),
        scratch_shapes=[
            pltpu.VMEM((L + 2 * SUBLANES, SSD_WIDTH), F32),
            pltpu.VMEM((L + 2 * SUBLANES, SSD_BC), F32),
            pltpu.VMEM((SSD_HEADS // 2, SSD_STATE, LANES), F32),
        ],
        compiler_params=pltpu.CompilerParams(
            dimension_semantics=("parallel", "arbitrary"), vmem_limit_bytes=VMEM_LIMIT),
        name="ssd_scan",
    )(proj, proj, proj, proj, conv_w, conv_b, conv_w, conv_b, dt_bias, a_log, d_skip, gn)


def _out_proj_kernel(a_ref, y_ref, x_ref, ga_ref, gp_ref, w_ref, o_ref):
    a = a_ref[...]
    an = (a * _rms_scale(a) * ga_ref[...]).astype(BF16)
    o = jnp.dot(an, w_ref[0:ATTN_WIDTH, :], preferred_element_type=F32)
    o = o + jnp.dot(y_ref[...].astype(BF16), w_ref[ATTN_WIDTH:, :], preferred_element_type=F32)
    o_ref[...] = x_ref[...] + o * _rms_scale(o) * gp_ref[...]


def _out_proj(attn, y, x, g_attn, g_post, w, layer, *, tm=512):
    T, D = x.shape
    vec = lambda width: pl.BlockSpec((None, 1, width), lambda i: (layer, 0, 0))
    return pl.pallas_call(
        _out_proj_kernel,
        grid=(T // tm,),
        in_specs=[
            pl.BlockSpec((tm, ATTN_WIDTH), lambda i: (i, 0)),
            pl.BlockSpec((tm, SSD_WIDTH), lambda i: (i, 0)),
            pl.BlockSpec((tm, D), lambda i: (i, 0)),
            vec(ATTN_WIDTH), vec(D),
            pl.BlockSpec((None, ATTN_WIDTH + SSD_WIDTH, D), lambda i: (layer, 0, 0)),
        ],
        out_specs=pl.BlockSpec((tm, D), lambda i: (i, 0)),
        out_shape=jax.ShapeDtypeStruct((T, D), F32),
        compiler_params=pltpu.CompilerParams(
            dimension_semantics=("parallel",), vmem_limit_bytes=VMEM_LIMIT),
        name="out_proj",
    )(attn, y, x, g_attn, g_post, w)


def _mlp_kernel(x_ref, g1_ref, g2_ref, wu_ref, wd_ref, o_ref, h_ref, acc_ref):
    f = pl.program_id(1)

    @pl.when(f == 0)
    def _():
        x = x_ref[...]
        h_ref[...] = ((x * _rms_scale(x)) * g1_ref[...]).astype(BF16)
        acc_ref[...] = jnp.zeros(acc_ref.shape, F32)

    u = jnp.maximum(jnp.dot(h_ref[...], wu_ref[...], preferred_element_type=F32), 0.0)
    acc_ref[...] += jnp.dot((u * u).astype(BF16), wd_ref[...], preferred_element_type=F32)

    @pl.when(f == pl.num_programs(1) - 1)
    def _():
        acc = acc_ref[...]
        o_ref[...] = x_ref[...] + acc * _rms_scale(acc) * g2_ref[...]


def _mlp(x, g_pre, g_post, w_up, w_down, layer, *, tm=512, tf=1024):
    T, D = x.shape
    F = w_up.shape[-1]
    vec = pl.BlockSpec((None, 1, D), lambda i, f: (layer, 0, 0))
    return pl.pallas_call(
        _mlp_kernel,
        grid=(T // tm, F // tf),
        in_specs=[
            pl.BlockSpec((tm, D), lambda i, f: (i, 0)),
            vec, vec,
            pl.BlockSpec((None, D, tf), lambda i, f: (layer, 0, f)),
            pl.BlockSpec((None, tf, D), lambda i, f: (layer, f, 0)),
        ],
        out_specs=pl.BlockSpec((tm, D), lambda i, f: (i, 0)),
        out_shape=jax.ShapeDtypeStruct((T, D), F32),
        scratch_shapes=[pltpu.VMEM((tm, D), BF16), pltpu.VMEM((tm, D), F32)],
        compiler_params=pltpu.CompilerParams(
            dimension_semantics=("parallel", "arbitrary"), vmem_limit_bytes=VMEM_LIMIT),
        name="mlp",
    )(x, g_pre, g_post, w_up, w_down)


def _row(p):
    return p[:, None, :]


def _pad_lanes(p):
    return jnp.pad(p, ((0, 0), (0, LANES - p.shape[-1])))


def kernel(x, positions, pre_mix_norm, post_mix_norm, pre_mlp_norm, post_mlp_norm, w_in, conv_w, conv_b, dt_bias, a_log, d_skip, attn_out_norm, ssd_out_norm, w_out, w_up, w_down):
    B, S, D = x.shape
    depth = w_in.shape[0]
    T = B * S

    w_in_b = jnp.pad(w_in.astype(BF16), ((0, 0), (0, 0), (0, IN_PAD - IN_DIM)))
    w_out_b = w_out.astype(BF16)
    w_up_b = w_up.astype(BF16)
    w_down_b = w_down.astype(BF16)
    dt_bias_p = _row(_pad_lanes(dt_bias))
    a_log_p = _row(_pad_lanes(a_log))
    d_skip_x = _row(jnp.repeat(d_skip, SSD_HEAD_DIM, axis=-1))
    conv_b_r = _row(conv_b)
    g_pre_mix, g_post_mix = _row(pre_mix_norm), _row(post_mix_norm)
    g_pre_mlp, g_post_mlp = _row(pre_mlp_norm), _row(post_mlp_norm)
    g_attn, g_ssd = _row(attn_out_norm), _row(ssd_out_norm)

    c_tab, sa_tab, sb_tab = _rope_tables(positions)

    xf = x.reshape(T, D)
    for l in range(depth):
        proj = _norm_matmul(xf, g_pre_mix, w_in_b, l).reshape(B, S, IN_PAD)
        attn = _moba_attention(proj, c_tab, sa_tab, sb_tab)
        y = _ssd(proj, conv_w, conv_b_r, dt_bias_p, a_log_p, d_skip_x, g_ssd, l)
        xf = _out_proj(attn.reshape(T, ATTN_WIDTH), y.reshape(T, SSD_WIDTH), xf,
                       g_attn, g_post_mix, w_out_b, l)
        xf = _mlp(xf, g_pre_mlp, g_post_mlp, w_up_b, w_down_b, l)
    return xf.reshape(B, S, D)
```

```python
import functools
import math

import jax
import jax.numpy as jnp
from jax import lax
from jax.experimental import pallas as pl
from jax.experimental.pallas import tpu as pltpu

F32 = jnp.float32
BF16 = jnp.bfloat16

D_MODEL = 2048
ATTN_WIDTH = 1024
HEAD_DIM = 128
ATTN_HEADS = ATTN_WIDTH // HEAD_DIM
ROPE_DIM = HEAD_DIM // 4
ROPE_HALF = ROPE_DIM // 2
ROPE_THETA = 500000.0
MOBA_BLOCK = 256
MOBA_TOPK = 3
MOBA_HEADS_PER_STEP = 4

SSD_WIDTH = 1024
SSD_HEAD_DIM = 64
SSD_HEADS = SSD_WIDTH // SSD_HEAD_DIM
SSD_GROUPS = 2
SSD_STATE = 128
SSD_CONV = 4
SSD_CHUNK = 128
SSD_CHUNKS_PER_STEP = 2
SSD_BC = 2 * SSD_GROUPS * SSD_STATE
CONV_DIM = SSD_WIDTH + SSD_BC
IN_DIM = 3 * ATTN_WIDTH + SSD_WIDTH + CONV_DIM + SSD_HEADS
LANES = 128
SUBLANES = 8
IN_PAD = -(-IN_DIM // LANES) * LANES
D_FF = 4 * D_MODEL
NORM_EPS = 1e-6

OFF_Z = 3 * ATTN_WIDTH
OFF_X = OFF_Z + SSD_WIDTH
OFF_BC = OFF_X + SSD_WIDTH
OFF_DT = OFF_BC + SSD_BC

VMEM_LIMIT = 56 * 1024 * 1024

NT_DIMS = (((1,), (1,)), ((), ()))
LOG2E = math.log2(math.e)


def _rms_scale(v):
    return lax.rsqrt(jnp.mean(v * v, axis=-1, keepdims=True) + NORM_EPS)


def _silu(v):
    return v * (0.5 + 0.5 * jnp.tanh(0.5 * v))


def _rope_table_kernel(pos_ref, c_ref, sa_ref, sb_ref):
    pos = pos_ref[...].astype(F32)
    lane = lax.broadcasted_iota(jnp.int32, (1, LANES), 1)
    k = (lane & (ROPE_HALF - 1)).astype(F32)
    inv_freq = jnp.exp(k * (-math.log(ROPE_THETA) / ROPE_HALF))
    ang = pos * inv_freq
    cos = jnp.cos(ang)
    sin = jnp.sin(ang)
    in_rope = lane < ROPE_DIM
    c_ref[...] = jnp.where(in_rope, cos, 1.0)
    sa_ref[...] = jnp.where(lane < ROPE_HALF, -sin, 0.0)
    sb_ref[...] = jnp.where(jnp.logical_and(lane >= ROPE_HALF, in_rope), sin, 0.0)


def _rope_tables(positions):
    B, S = positions.shape
    spec = pl.BlockSpec((None, S, LANES), lambda b: (b, 0, 0))
    shape = jax.ShapeDtypeStruct((B, S, LANES), F32)
    return pl.pallas_call(
        _rope_table_kernel,
        grid=(B,),
        in_specs=[pl.BlockSpec((None, S, 1), lambda b: (b, 0, 0))],
        out_specs=[spec, spec, spec],
        out_shape=[shape, shape, shape],
        name="rope_tables",
    )(positions.reshape(B, S, 1))


def _norm_matmul_kernel(x_ref, g_ref, w_ref, o_ref, h_ref):
    @pl.when(pl.program_id(1) == 0)
    def _():
        x = x_ref[...]
        h_ref[...] = ((x * _rms_scale(x)) * g_ref[...]).astype(BF16)

    o_ref[...] = jnp.dot(h_ref[...], w_ref[...], preferred_element_type=F32)


def _norm_matmul(x, gain, w, layer, *, tm=1024, tn=1920):
    T, D = x.shape
    N = w.shape[-1]
    return pl.pallas_call(
        _norm_matmul_kernel,
        grid=(T // tm, N // tn),
        in_specs=[
            pl.BlockSpec((tm, D), lambda i, j: (i, 0)),
            pl.BlockSpec((None, 1, D), lambda i, j: (layer, 0, 0)),
            pl.BlockSpec((None, D, tn), lambda i, j: (layer, 0, j)),
        ],
        out_specs=pl.BlockSpec((tm, tn), lambda i, j: (i, j)),
        out_shape=jax.ShapeDtypeStruct((T, N), F32),
        scratch_shapes=[pltpu.VMEM((tm, D), BF16)],
        compiler_params=pltpu.CompilerParams(
            dimension_semantics=("parallel", "arbitrary"), vmem_limit_bytes=VMEM_LIMIT),
        name="norm_in_proj",
    )(x, gain, w)


def _split_bf16(t):
    hi = t.astype(BF16)
    return hi, (t - hi.astype(F32)).astype(BF16)


def _moba_kernel(q_ref, k_ref, v_ref, c_ref, sa_ref, sb_ref, o_ref,
                 kr_ref, vt_ref, km_ref, *, n_blocks):
    i = pl.program_id(2)
    blk = MOBA_BLOCK

    def rope(t, rows):
        return (t * c_ref[rows, :]
                + pltpu.roll(t, LANES - ROPE_HALF, 1) * sa_ref[rows, :]
                + pltpu.roll(t, ROPE_HALF, 1) * sb_ref[rows, :])

    heads = range(kr_ref.shape[0])
    cols = lambda hh: slice(hh * HEAD_DIM, (hh + 1) * HEAD_DIM)

    @pl.when(i == 0)
    def _():
        for hh in heads:
            for b in range(n_blocks):
                rows = pl.ds(b * blk, blk)
                kb = rope(k_ref[rows, cols(hh)], rows)
                kr_ref[hh, rows, :] = kb.astype(BF16)
                km_ref[hh, b:b + 1, :] = jnp.mean(kb, axis=0, keepdims=True)
                vt_ref[hh, :, b * blk:(b + 1) * blk] = v_ref[rows, cols(hh)].T.astype(BF16)

    q_rows = pl.ds(pl.multiple_of(i * blk, blk), blk)
    q_atts, biases = [], []
    for hh in heads:
        q = rope(q_ref[:, cols(hh)], q_rows) * (HEAD_DIM ** -0.5)
        q_hi, q_lo = _split_bf16(q)
        km_hi, km_lo = _split_bf16(km_ref[hh])
        gate = lax.dot_general(jnp.concatenate([km_hi, km_lo, km_hi], axis=1),
                               jnp.concatenate([q_hi, q_hi, q_lo], axis=1),
                               NT_DIMS, preferred_element_type=F32)
        jidx = lax.broadcasted_iota(jnp.int32, gate.shape, 0)
        past = jidx < i
        gate = jnp.where(past, gate, -jnp.inf)
        rank = jnp.zeros(gate.shape, jnp.int32)
        for m in range(n_blocks):
            gm = gate[m:m + 1, :]
            beats = jnp.logical_or(gm > gate, jnp.logical_and(gm == gate, m < jidx))
            rank = rank + beats.astype(jnp.int32)
        q_atts.append((q * LOG2E).astype(BF16))
        selected = jnp.logical_and(past, rank < MOBA_TOPK)
        biases.append(jnp.where(selected, 0.0, -jnp.inf))

    key_i = lax.broadcasted_iota(jnp.int32, (blk, blk), 0)
    qry_i = lax.broadcasted_iota(jnp.int32, (blk, blk), 1)
    causal_bias = jnp.where(key_i <= qry_i, 0.0, -jnp.inf)

    def attend(n):
        nk = (n + 1) * blk
        for hh in heads:
            bias = biases[hh]
            s = lax.dot_general(kr_ref[hh, 0:nk, :], q_atts[hh], NT_DIMS,
                                preferred_element_type=F32)
            parts = [s[j * blk:(j + 1) * blk, :] + bias[j:j + 1, :] for j in range(n)]
            parts.append(s[n * blk:nk, :] + causal_bias)
            m_max = functools.reduce(jnp.maximum, [jnp.max(t, axis=0, keepdims=True) for t in parts])
            probs = [jnp.exp2(t - m_max) for t in parts]
            denom = functools.reduce(jnp.add, [jnp.sum(t, axis=0, keepdims=True) for t in probs])
            acc = jnp.dot(vt_ref[hh, :, 0:nk], jnp.concatenate(probs, axis=0).astype(BF16),
                          preferred_element_type=F32)
            o_ref[:, cols(hh)] = (acc * (1.0 / denom)).T

    for n in range(n_blocks):
        pl.when(i == n)(functools.partial(attend, n))


def _moba_attention(proj, c_tab, sa_tab, sb_tab):
    B, S, _ = proj.shape
    nb = S // MOBA_BLOCK
    hp = MOBA_HEADS_PER_STEP
    G = ATTN_HEADS // hp
    width = hp * HEAD_DIM
    full = lambda off: pl.BlockSpec((None, S, width), lambda b, h, i: (b, 0, off + h))
    tab = pl.BlockSpec((None, S, LANES), lambda b, h, i: (b, 0, 0))
    return pl.pallas_call(
        functools.partial(_moba_kernel, n_blocks=nb),
        grid=(B, G, nb),
        in_specs=[
            pl.BlockSpec((None, MOBA_BLOCK, width), lambda b, h, i: (b, i, h)),
            full(G), full(2 * G), tab, tab, tab,
        ],
        out_specs=pl.BlockSpec((None, MOBA_BLOCK, width), lambda b, h, i: (b, i, h)),
        out_shape=jax.ShapeDtypeStruct((B, S, ATTN_WIDTH), F32),
        scratch_shapes=[
            pltpu.VMEM((hp, S, HEAD_DIM), BF16),
            pltpu.VMEM((hp, HEAD_DIM, S), BF16),
            pltpu.VMEM((hp, nb, HEAD_DIM), F32),
        ],
        compiler_params=pltpu.CompilerParams(
            dimension_semantics=("parallel", "parallel", "arbitrary"), vmem_limit_bytes=VMEM_LIMIT),
        name="moba_attention",
    )(proj, proj, proj, c_tab, sa_tab, sb_tab)


def _ssd_kernel(z_ref, xs_ref, bc_ref, dt_ref, wx_ref, bx_ref, wbc_ref, bbc_ref,
                dtb_ref, alog_ref, dskip_ref, gn_ref, o_ref,
                extx_ref, extbc_ref, state_ref):
    L = SSD_CHUNK
    N = SSD_STATE
    halo = SUBLANES

    @pl.when(pl.program_id(1) == 0)
    def _():
        extx_ref[0:halo, :] = jnp.zeros((halo, SSD_WIDTH), F32)
        extbc_ref[0:halo, :] = jnp.zeros((halo, SSD_BC), F32)
        state_ref[...] = jnp.zeros(state_ref.shape, F32)

    def chunk(rows):
        def conv_silu(ext_ref, cur_ref, w_ref, b_ref):
            ext_ref[halo:halo + L, :] = cur_ref[rows, :]
            acc = b_ref[...] + ext_ref[pl.ds(halo, L), :] * w_ref[SSD_CONV - 1:SSD_CONV, :]
            for k in range(SSD_CONV - 1):
                acc = acc + ext_ref[pl.ds(halo - (SSD_CONV - 1) + k, L), :] * w_ref[k:k + 1, :]
            ext_ref[0:halo, :] = ext_ref[L:L + halo, :]
            return _silu(acc)

        xc = conv_silu(extx_ref, xs_ref, wx_ref, bx_ref)
        bcc = conv_silu(extbc_ref, bc_ref, wbc_ref, bbc_ref)

        dt_in = dt_ref[rows, :] + dtb_ref[...]
        dt = jnp.maximum(dt_in, 0.0) + jnp.log(1.0 + jnp.exp(-jnp.abs(dt_in)))
        a_dt = dt * (-jnp.exp(alog_ref[...]))
        row = lax.broadcasted_iota(jnp.int32, (L, L), 0)
        col = lax.broadcasted_iota(jnp.int32, (L, L), 1)
        tri = row >= col
        a_cs = jnp.dot(tri.astype(F32), a_dt, precision=lax.Precision.HIGHEST,
                       preferred_element_type=F32)
        a_cs_t = a_cs.T[0:SSD_HEADS, :]
        dt_t = dt.T[0:SSD_HEADS, :]
        a_last = a_cs_t[:, L - 1:L]
        w_t = dt_t * jnp.exp(a_last - a_cs_t)
        chunk_decay = jnp.broadcast_to(jnp.exp(a_last), (SSD_HEADS, LANES))

        lane = lax.broadcasted_iota(jnp.int32, (1, LANES), 1)
        lo = lane < SSD_HEAD_DIM

        def block_diag(v):
            return jnp.concatenate([jnp.where(lo, v, 0.0), jnp.where(lo, 0.0, v)], axis=0).astype(BF16)

        heads_per_group = SSD_HEADS // SSD_GROUPS
        y_parts = []
        for g in range(SSD_GROUPS):
            b_g = bcc[:, g * N:(g + 1) * N]
            c_g = bcc[:, (SSD_GROUPS + g) * N:(SSD_GROUPS + g + 1) * N]
            cb = lax.dot_general(c_g.astype(BF16), b_g.astype(BF16), NT_DIMS,
                                 preferred_element_type=F32)
            b_t = b_g.T
            for pr in range(heads_per_group // 2):
                m_parts, cs_parts, bw_parts = [], [], []
                for e in (g * heads_per_group + 2 * pr, g * heads_per_group + 2 * pr + 1):
                    colb = jnp.broadcast_to(a_cs[:, e:e + 1], (L, L))
                    decay = jnp.where(tri, jnp.exp(colb - a_cs_t[e:e + 1, :]), 0.0)
                    m_parts.append((cb * decay * dt_t[e:e + 1, :]).astype(BF16))
                    cs_parts.append((c_g * jnp.exp(colb)).astype(BF16))
                    bw_parts.append((b_t * w_t[e:e + 1, :]).astype(BF16))
                pair = g * (heads_per_group // 2) + pr
                e0 = 2 * pair
                x_bd = block_diag(xc[:, pair * LANES:(pair + 1) * LANES])
                st = state_ref[pair]
                rhs = jnp.concatenate([x_bd, block_diag(st)], axis=0)
                y_parts.append(jnp.dot(jnp.concatenate(m_parts + cs_parts, axis=1), rhs,
                                       preferred_element_type=F32))
                st_new = jnp.dot(jnp.concatenate(bw_parts, axis=1), x_bd,
                                 preferred_element_type=F32)
                dec = jnp.where(lo, chunk_decay[e0:e0 + 1, :], chunk_decay[e0 + 1:e0 + 2, :])
                state_ref[pair] = st * dec + st_new

        y = jnp.concatenate(y_parts, axis=1) + dskip_ref[...] * xc
        y = y * _silu(z_ref[rows, :])
        gw = SSD_WIDTH // SSD_GROUPS
        outs = []
        for g in range(SSD_GROUPS):
            yg = y[:, g * gw:(g + 1) * gw]
            outs.append(yg * _rms_scale(yg) * gn_ref[:, g * gw:(g + 1) * gw])
        o_ref[rows, :] = jnp.concatenate(outs, axis=1)

    for sub in range(o_ref.shape[0] // L):
        chunk(pl.ds(sub * L, L))


def _ssd(proj, conv_w, conv_b, dt_bias, a_log, d_skip, gn, layer):
    B, S, _ = proj.shape
    L = SSD_CHUNKS_PER_STEP * SSD_CHUNK
    col = lambda width, off: pl.BlockSpec((None, L, width), lambda b, c: (b, c, off // width))
    par = lambda rows, width, off: pl.BlockSpec((None, rows, width), lambda b, c: (layer, 0, off // width))
    return pl.pallas_call(
        _ssd_kernel,
        grid=(B, S // L),
        in_specs=[
            col(SSD_WIDTH, OFF_Z), col(SSD_WIDTH, OFF_X), col(SSD_BC, OFF_BC), col(LANES, OFF_DT),
            par(SSD_CONV, SSD_WIDTH, 0), par(1, SSD_WIDTH, 0),
            par(SSD_CONV, SSD_BC, SSD_WIDTH), par(1, SSD_BC, SSD_WIDTH),
            par(1, LANES, 0), par(1, LANES, 0), par(1, SSD_WIDTH, 0), par(1, SSD_WIDTH, 0),
        ],
        out_specs=pl.BlockSpec((None, L, SSD_WIDTH), lambda b, c: (b, c, 0)),
        out_shape=jax.ShapeDtypeStruct((B, S, SSD_WIDTH), F32),
        scratch_shapes=[
            pltpu.VMEM((SSD_CHUNK + SUBLANES, SSD_WIDTH), F32),
            pltpu.VMEM((SSD_CHUNK + SUBLANES, SSD_BC), F32),
            pltpu.VMEM((SSD_HEADS // 2, SSD_STATE, LANES), F32),
        ],
        compiler_params=pltpu.CompilerParams(
            dimension_semantics=("parallel", "arbitrary"), vmem_limit_bytes=VMEM_LIMIT),
        name="ssd_scan",
    )(proj, proj, proj, proj, conv_w, conv_b, conv_w, conv_b, dt_bias, a_log, d_skip, gn)


def _out_proj_kernel(a_ref, y_ref, x_ref, ga_ref, gp_ref, w_ref, o_ref):
    a = a_ref[...]
    an = (a * _rms_scale(a) * ga_ref[...]).astype(BF16)
    o = jnp.dot(an, w_ref[0:ATTN_WIDTH, :], preferred_element_type=F32)
    o = o + jnp.dot(y_ref[...].astype(BF16), w_ref[ATTN_WIDTH:, :], preferred_element_type=F32)
    o_ref[...] = x_ref[...] + o * _rms_scale(o) * gp_ref[...]


def _out_proj(attn, y, x, g_attn, g_post, w, layer, *, tm=512):
    T, D = x.shape
    vec = lambda width: pl.BlockSpec((None, 1, width), lambda i: (layer, 0, 0))
    return pl.pallas_call(
        _out_proj_kernel,
        grid=(T // tm,),
        in_specs=[
            pl.BlockSpec((tm, ATTN_WIDTH), lambda i: (i, 0)),
            pl.BlockSpec((tm, SSD_WIDTH), lambda i: (i, 0)),
            pl.BlockSpec((tm, D), lambda i: (i, 0)),
            vec(ATTN_WIDTH), vec(D),
            pl.BlockSpec((None, ATTN_WIDTH + SSD_WIDTH, D), lambda i: (layer, 0, 0)),
        ],
        out_specs=pl.BlockSpec((tm, D), lambda i: (i, 0)),
        out_shape=jax.ShapeDtypeStruct((T, D), F32),
        compiler_params=pltpu.CompilerParams(
            dimension_semantics=("parallel",), vmem_limit_bytes=VMEM_LIMIT),
        name="out_proj",
    )(attn, y, x, g_attn, g_post, w)


def _mlp_kernel(x_ref, g1_ref, g2_ref, wu_ref, wd_ref, o_ref, h_ref, acc_ref):
    f = pl.program_id(1)

    @pl.when(f == 0)
    def _():
        x = x_ref[...]
        h_ref[...] = ((x * _rms_scale(x)) * g1_ref[...]).astype(BF16)
        acc_ref[...] = jnp.zeros(acc_ref.shape, F32)

    u = jnp.maximum(jnp.dot(h_ref[...], wu_ref[...], preferred_element_type=F32), 0.0)
    acc_ref[...] += jnp.dot((u * u).astype(BF16), wd_ref[...], preferred_element_type=F32)

    @pl.when(f == pl.num_programs(1) - 1)
    def _():
        acc = acc_ref[...]
        o_ref[...] = x_ref[...] + acc * _rms_scale(acc) * g2_ref[...]


def _mlp(x, g_pre, g_post, w_up, w_down, layer, *, tm=512, tf=1024):
    T, D = x.shape
    F = w_up.shape[-1]
    vec = pl.BlockSpec((None, 1, D), lambda i, f: (layer, 0, 0))
    return pl.pallas_call(
        _mlp_kernel,
        grid=(T // tm, F // tf),
        in_specs=[
            pl.BlockSpec((tm, D), lambda i, f: (i, 0)),
            vec, vec,
            pl.BlockSpec((None, D, tf), lambda i, f: (layer, 0, f)),
            pl.BlockSpec((None, tf, D), lambda i, f: (layer, f, 0)),
        ],
        out_specs=pl.BlockSpec((tm, D), lambda i, f: (i, 0)),
        out_shape=jax.ShapeDtypeStruct((T, D), F32),
        scratch_shapes=[pltpu.VMEM((tm, D), BF16), pltpu.VMEM((tm, D), F32)],
        compiler_params=pltpu.CompilerParams(
            dimension_semantics=("parallel", "arbitrary"), vmem_limit_bytes=VMEM_LIMIT),
        name="mlp",
    )(x, g_pre, g_post, w_up, w_down)


def _row(p):
    return p[:, None, :]


def _pad_lanes(p):
    return jnp.pad(p, ((0, 0), (0, LANES - p.shape[-1])))


def kernel(x, positions, pre_mix_norm, post_mix_norm, pre_mlp_norm, post_mlp_norm, w_in, conv_w, conv_b, dt_bias, a_log, d_skip, attn_out_norm, ssd_out_norm, w_out, w_up, w_down):
    B, S, D = x.shape
    depth = w_in.shape[0]
    T = B * S

    w_in_b = jnp.concatenate(
        [w_in.astype(BF16), jnp.zeros((depth, D, IN_PAD - IN_DIM), BF16)], axis=-1)
    w_out_b = w_out.astype(BF16)
    w_up_b = w_up.astype(BF16)
    w_down_b = w_down.astype(BF16)
    dt_bias_p = _row(_pad_lanes(dt_bias))
    a_log_p = _row(_pad_lanes(a_log))
    d_skip_x = _row(jnp.repeat(d_skip, SSD_HEAD_DIM, axis=-1))
    conv_b_r = _row(conv_b)
    g_pre_mix, g_post_mix = _row(pre_mix_norm), _row(post_mix_norm)
    g_pre_mlp, g_post_mlp = _row(pre_mlp_norm), _row(post_mlp_norm)
    g_attn, g_ssd = _row(attn_out_norm), _row(ssd_out_norm)

    c_tab, sa_tab, sb_tab = _rope_tables(positions)

    xf = x.reshape(T, D)
    for l in range(depth):
        proj = _norm_matmul(xf, g_pre_mix, w_in_b, l).reshape(B, S, IN_PAD)
        attn = _moba_attention(proj, c_tab, sa_tab, sb_tab)
        y = _ssd(proj, conv_w, conv_b_r, dt_bias_p, a_log_p, d_skip_x, g_ssd, l)
        xf = _out_proj(attn.reshape(T, ATTN_WIDTH), y.reshape(T, SSD_WIDTH), xf,
                       g_attn, g_post_mix, w_out_b, l)
        xf = _mlp(xf, g_pre_mlp, g_post_mlp, w_up_b, w_down_b, l)
    return xf.reshape(B, S, D)
```

```python
import functools
import math

import jax
import jax.numpy as jnp
from jax import lax
from jax.experimental import pallas as pl
from jax.experimental.pallas import tpu as pltpu

F32 = jnp.float32
BF16 = jnp.bfloat16

D_MODEL = 2048
ATTN_WIDTH = 1024
HEAD_DIM = 128
ATTN_HEADS = ATTN_WIDTH // HEAD_DIM
ROPE_DIM = HEAD_DIM // 4
ROPE_HALF = ROPE_DIM // 2
ROPE_THETA = 500000.0
MOBA_BLOCK = 256
MOBA_TOPK = 3
MOBA_HEADS_PER_STEP = 4

SSD_WIDTH = 1024
SSD_HEAD_DIM = 64
SSD_HEADS = SSD_WIDTH // SSD_HEAD_DIM
SSD_GROUPS = 2
SSD_STATE = 128
SSD_CONV = 4
SSD_CHUNK = 128
SSD_CHUNKS_PER_STEP = 2
SSD_BC = 2 * SSD_GROUPS * SSD_STATE
CONV_DIM = SSD_WIDTH + SSD_BC
IN_DIM = 3 * ATTN_WIDTH + SSD_WIDTH + CONV_DIM + SSD_HEADS
LANES = 128
SUBLANES = 8
IN_PAD = -(-IN_DIM // LANES) * LANES
D_FF = 4 * D_MODEL
NORM_EPS = 1e-6

OFF_Z = 3 * ATTN_WIDTH
OFF_X = OFF_Z + SSD_WIDTH
OFF_BC = OFF_X + SSD_WIDTH
OFF_DT = OFF_BC + SSD_BC

VMEM_LIMIT = 56 * 1024 * 1024

NT_DIMS = (((1,), (1,)), ((), ()))
LOG2E = math.log2(math.e)


def _rms_scale(v):
    return lax.rsqrt(jnp.mean(v * v, axis=-1, keepdims=True) + NORM_EPS)


def _silu(v):
    return v * (0.5 + 0.5 * jnp.tanh(0.5 * v))


def _rope_table_kernel(pos_ref, c_ref, s_ref):
    pos = pos_ref[...].astype(F32)
    lane = lax.broadcasted_iota(jnp.int32, (1, LANES), 1)
    k = (lane & (ROPE_HALF - 1)).astype(F32)
    inv_freq = jnp.exp(k * (-math.log(ROPE_THETA) / ROPE_HALF))
    ang = pos * inv_freq
    cos = jnp.cos(ang)
    sin = jnp.sin(ang)
    in_rope = lane < ROPE_DIM
    c_ref[...] = jnp.where(in_rope, cos, 1.0)
    s_ref[...] = jnp.where(lane < ROPE_HALF, -sin, jnp.where(in_rope, sin, 0.0))


def _rope_tables(positions):
    B, S = positions.shape
    spec = pl.BlockSpec((None, S, LANES), lambda b: (b, 0, 0))
    shape = jax.ShapeDtypeStruct((B, S, LANES), F32)
    return pl.pallas_call(
        _rope_table_kernel,
        grid=(B,),
        in_specs=[pl.BlockSpec((None, S, 1), lambda b: (b, 0, 0))],
        out_specs=[spec, spec],
        out_shape=[shape, shape],
        name="rope_tables",
    )(positions.reshape(B, S, 1))


def _norm_matmul_kernel(x_ref, g_ref, w_ref, o_ref, h_ref):
    @pl.when(pl.program_id(1) == 0)
    def _():
        x = x_ref[...]
        h_ref[...] = ((x * _rms_scale(x)) * g_ref[...]).astype(BF16)

    o_ref[...] = jnp.dot(h_ref[...], w_ref[...], preferred_element_type=F32)


def _norm_matmul(x, gain, w, layer, *, tm=1024, tn=1920):
    T, D = x.shape
    N = w.shape[-1]
    return pl.pallas_call(
        _norm_matmul_kernel,
        grid=(T // tm, N // tn),
        in_specs=[
            pl.BlockSpec((tm, D), lambda i, j: (i, 0)),
            pl.BlockSpec((None, 1, D), lambda i, j: (layer, 0, 0)),
            pl.BlockSpec((None, D, tn), lambda i, j: (layer, 0, j)),
        ],
        out_specs=pl.BlockSpec((tm, tn), lambda i, j: (i, j)),
        out_shape=jax.ShapeDtypeStruct((T, N), F32),
        scratch_shapes=[pltpu.VMEM((tm, D), BF16)],
        compiler_params=pltpu.CompilerParams(
            dimension_semantics=("parallel", "arbitrary"), vmem_limit_bytes=VMEM_LIMIT),
        name="norm_in_proj",
    )(x, gain, w)


def _split_bf16(t):
    hi = t.astype(BF16)
    return hi, (t - hi.astype(F32)).astype(BF16)


def _moba_kernel(q_ref, k_ref, v_ref, c_ref, s_ref, o_ref,
                 kr_ref, vt_ref, qa_ref, bias_ref, *, n_blocks):
    i = pl.program_id(2)
    blk = MOBA_BLOCK

    def rope(t, rows):
        lane = lax.broadcasted_iota(jnp.int32, (1, LANES), 1)
        swapped = jnp.where(lane < ROPE_HALF, pltpu.roll(t, LANES - ROPE_HALF, 1), pltpu.roll(t, ROPE_HALF, 1))
        return t * c_ref[rows, :] + swapped * s_ref[rows, :]

    heads = range(kr_ref.shape[0])
    cols = lambda hh: slice(hh * HEAD_DIM, (hh + 1) * HEAD_DIM)

    @pl.when(i == 0)
    def _():
        for hh in heads:
            k_means = []
            for b in range(n_blocks):
                rows = pl.ds(b * blk, blk)
                kb = rope(k_ref[rows, cols(hh)], rows)
                kr_ref[hh, rows, :] = kb.astype(BF16)
                k_means.append(jnp.mean(kb, axis=0, keepdims=True))
                vt_ref[hh, :, b * blk:(b + 1) * blk] = v_ref[rows, cols(hh)].astype(BF16).T
            km_hi, km_lo = _split_bf16(jnp.concatenate(k_means, axis=0))
            km_cat = jnp.concatenate([km_hi, km_lo, km_hi], axis=1)
            for qb in range(n_blocks):
                rows = pl.ds(qb * blk, blk)
                q = rope(q_ref[rows, cols(hh)], rows) * (HEAD_DIM ** -0.5)
                q_hi, q_lo = _split_bf16(q)
                gate = lax.dot_general(km_cat, jnp.concatenate([q_hi, q_hi, q_lo], axis=1),
                                       NT_DIMS, preferred_element_type=F32)
                jidx = lax.broadcasted_iota(jnp.int32, gate.shape, 0)
                past = jidx < qb
                gate = jnp.where(past, gate, -jnp.inf)
                rank = jnp.zeros(gate.shape, jnp.int32)
                for m in range(n_blocks):
                    gm = gate[m:m + 1, :]
                    beats = jnp.logical_or(gm > gate, jnp.logical_and(gm == gate, m < jidx))
                    rank = rank + beats.astype(jnp.int32)
                selected = jnp.logical_and(past, rank < MOBA_TOPK)
                bias_ref[hh, qb] = jnp.where(selected, 0.0, -jnp.inf)
                qa_ref[hh, qb] = (q * LOG2E).astype(BF16)

    q_atts = [qa_ref[hh, i] for hh in heads]
    biases = [bias_ref[hh, i] for hh in heads]

    key_i = lax.broadcasted_iota(jnp.int32, (blk, blk), 0)
    qry_i = lax.broadcasted_iota(jnp.int32, (blk, blk), 1)
    causal_bias = jnp.where(key_i <= qry_i, 0.0, -jnp.inf)

    def attend(n):
        nk = (n + 1) * blk
        for hh in heads:
            bias = biases[hh]
            s = lax.dot_general(kr_ref[hh, 0:nk, :], q_atts[hh], NT_DIMS,
                                preferred_element_type=F32)
            parts = [s[j * blk:(j + 1) * blk, :] + bias[j:j + 1, :] for j in range(n)]
            parts.append(s[n * blk:nk, :] + causal_bias)
            m_max = functools.reduce(jnp.maximum, [jnp.max(t, axis=0, keepdims=True) for t in parts])
            probs = [jnp.exp2(t - m_max) for t in parts]
            denom = functools.reduce(jnp.add, [jnp.sum(t, axis=0, keepdims=True) for t in probs])
            acc = jnp.dot(vt_ref[hh, :, 0:nk], jnp.concatenate(probs, axis=0).astype(BF16),
                          preferred_element_type=F32)
            o_ref[:, cols(hh)] = (acc * (1.0 / denom)).T

    for n in range(n_blocks):
        pl.when(i == n)(functools.partial(attend, n))


def _moba_attention(proj, c_tab, s_tab):
    B, S, _ = proj.shape
    nb = S // MOBA_BLOCK
    hp = MOBA_HEADS_PER_STEP
    G = ATTN_HEADS // hp
    width = hp * HEAD_DIM
    full = lambda off: pl.BlockSpec((None, S, width), lambda b, h, i: (b, 0, off + h))
    tab = pl.BlockSpec((None, S, LANES), lambda b, h, i: (b, 0, 0))
    return pl.pallas_call(
        functools.partial(_moba_kernel, n_blocks=nb),
        grid=(B, G, nb),
        in_specs=[full(0), full(G), full(2 * G), tab, tab],
        out_specs=pl.BlockSpec((None, MOBA_BLOCK, width), lambda b, h, i: (b, i, h)),
        out_shape=jax.ShapeDtypeStruct((B, S, ATTN_WIDTH), F32),
        scratch_shapes=[
            pltpu.VMEM((hp, S, HEAD_DIM), BF16),
            pltpu.VMEM((hp, HEAD_DIM, S), BF16),
            pltpu.VMEM((hp, nb, MOBA_BLOCK, HEAD_DIM), BF16),
            pltpu.VMEM((hp, nb, nb, MOBA_BLOCK), F32),
        ],
        compiler_params=pltpu.CompilerParams(
            dimension_semantics=("parallel", "parallel", "arbitrary"), vmem_limit_bytes=VMEM_LIMIT),
        name="moba_attention",
    )(proj, proj, proj, c_tab, s_tab)


def _ssd_kernel(z_ref, xs_ref, bc_ref, dt_ref, wx_ref, bx_ref, wbc_ref, bbc_ref,
                dtb_ref, alog_ref, dskip_ref, gn_ref, o_ref,
                extx_ref, extbc_ref, state_ref):
    L = SSD_CHUNK
    N = SSD_STATE
    halo = SUBLANES

    @pl.when(pl.program_id(1) == 0)
    def _():
        extx_ref[0:halo, :] = jnp.zeros((halo, SSD_WIDTH), F32)
        extbc_ref[0:halo, :] = jnp.zeros((halo, SSD_BC), F32)
        state_ref[...] = jnp.zeros(state_ref.shape, F32)

    def chunk(rows):
        def conv_silu(ext_ref, cur_ref, w_ref, b_ref):
            ext_ref[halo:halo + L, :] = cur_ref[rows, :]
            acc = b_ref[...] + ext_ref[pl.ds(halo, L), :] * w_ref[SSD_CONV - 1:SSD_CONV, :]
            for k in range(SSD_CONV - 1):
                acc = acc + ext_ref[pl.ds(halo - (SSD_CONV - 1) + k, L), :] * w_ref[k:k + 1, :]
            ext_ref[0:halo, :] = ext_ref[L:L + halo, :]
            return _silu(acc)

        xc = conv_silu(extx_ref, xs_ref, wx_ref, bx_ref)
        bcc = conv_silu(extbc_ref, bc_ref, wbc_ref, bbc_ref)

        dt_in = dt_ref[rows, :] + dtb_ref[...]
        dt = jnp.maximum(dt_in, 0.0) + jnp.log(1.0 + jnp.exp(-jnp.abs(dt_in)))
        a_dt = dt * (-jnp.exp(alog_ref[...]))
        row = lax.broadcasted_iota(jnp.int32, (L, L), 0)
        col = lax.broadcasted_iota(jnp.int32, (L, L), 1)
        tri = row >= col
        a_cs = jnp.dot(tri.astype(F32), a_dt, precision=lax.Precision.HIGHEST,
                       preferred_element_type=F32)
        a_cs_t = a_cs.T[0:SSD_HEADS, :]
        dt_t = dt.T[0:SSD_HEADS, :]
        a_last = a_cs_t[:, L - 1:L]
        w_t = dt_t * jnp.exp(a_last - a_cs_t)
        chunk_decay = jnp.broadcast_to(jnp.exp(a_last), (SSD_HEADS, LANES))

        lane = lax.broadcasted_iota(jnp.int32, (1, LANES), 1)
        lo = lane < SSD_HEAD_DIM

        def block_diag(v):
            return jnp.concatenate([jnp.where(lo, v, 0.0), jnp.where(lo, 0.0, v)], axis=0).astype(BF16)

        heads_per_group = SSD_HEADS // SSD_GROUPS
        y_parts = []
        for g in range(SSD_GROUPS):
            b_g = bcc[:, g * N:(g + 1) * N]
            c_g = bcc[:, (SSD_GROUPS + g) * N:(SSD_GROUPS + g + 1) * N]
            cb = lax.dot_general(c_g.astype(BF16), b_g.astype(BF16), NT_DIMS,
                                 preferred_element_type=F32)
            b_t = b_g.T
            for pr in range(heads_per_group // 2):
                m_parts, cs_parts, bw_parts = [], [], []
                for e in (g * heads_per_group + 2 * pr, g * heads_per_group + 2 * pr + 1):
                    colb = jnp.broadcast_to(a_cs[:, e:e + 1], (L, L))
                    decay = jnp.where(tri, jnp.exp(colb - a_cs_t[e:e + 1, :]), 0.0)
                    m_parts.append((cb * decay * dt_t[e:e + 1, :]).astype(BF16))
                    cs_parts.append((c_g * jnp.exp(colb)).astype(BF16))
                    bw_parts.append((b_t * w_t[e:e + 1, :]).astype(BF16))
                pair = g * (heads_per_group // 2) + pr
                e0 = 2 * pair
                x_bd = block_diag(xc[:, pair * LANES:(pair + 1) * LANES])
                st = state_ref[pair]
                rhs = jnp.concatenate([x_bd, block_diag(st)], axis=0)
                y_parts.append(jnp.dot(jnp.concatenate(m_parts + cs_parts, axis=1), rhs,
                                       preferred_element_type=F32))
                st_new = jnp.dot(jnp.concatenate(bw_parts, axis=1), x_bd,
                                 preferred_element_type=F32)
                dec = jnp.where(lo, chunk_decay[e0:e0 + 1, :], chunk_decay[e0 + 1:e0 + 2, :])
                state_ref[pair] = st * dec + st_new

        y = jnp.concatenate(y_parts, axis=1) + dskip_ref[...] * xc
        y = y * _silu(z_ref[rows, :])
        gw = SSD_WIDTH // SSD_GROUPS
        outs = []
        for g in range(SSD_GROUPS):
            yg = y[:, g * gw:(g + 1) * gw]
            outs.append(yg * _rms_scale(yg) * gn_ref[:, g * gw:(g + 1) * gw])
        o_ref[rows, :] = jnp.concatenate(outs, axis=1)

    for sub in range(o_ref.shape[0] // L):
        chunk(pl.ds(sub * L, L))


def _ssd(proj, conv_w, conv_b, dt_bias, a_log, d_skip, gn, layer):
    B, S, _ = proj.shape
    L = SSD_CHUNKS_PER_STEP * SSD_CHUNK
    col = lambda width, off: pl.BlockSpec((None, L, width), lambda b, c: (b, c, off // width))
    par = lambda rows, width, off: pl.BlockSpec((None, rows, width), lambda b, c: (layer, 0, off // width))
    return pl.pallas_call(
        _ssd_kernel,
        grid=(B, S // L),
        in_specs=[
            col(SSD_WIDTH, OFF_Z), col(SSD_WIDTH, OFF_X), col(SSD_BC, OFF_BC), col(LANES, OFF_DT),
            par(SSD_CONV, SSD_WIDTH, 0), par(1, SSD_WIDTH, 0),
            par(SSD_CONV, SSD_BC, SSD_WIDTH), par(1, SSD_BC, SSD_WIDTH),
            par(1, LANES, 0), par(1, LANES, 0), par(1, SSD_WIDTH, 0), par(1, SSD_WIDTH, 0),
        ],
        out_specs=pl.BlockSpec((None, L, SSD_WIDTH), lambda b, c: (b, c, 0)),
        out_shape=jax.ShapeDtypeStruct((B, S, SSD_WIDTH), F32),
        scratch_shapes=[
            pltpu.VMEM((SSD_CHUNK + SUBLANES, SSD_WIDTH), F32),
            pltpu.VMEM((SSD_CHUNK + SUBLANES, SSD_BC), F32),
            pltpu.VMEM((SSD_HEADS // 2, SSD_STATE, LANES), F32),
        ],
        compiler_params=pltpu.CompilerParams(
            dimension_semantics=("parallel", "arbitrary"), vmem_limit_bytes=VMEM_LIMIT),
        name="ssd_scan",
    )(proj, proj, proj, proj, conv_w, conv_b, conv_w, conv_b, dt_bias, a_log, d_skip, gn)


def _out_proj_kernel(a_ref, y_ref, x_ref, ga_ref, gp_ref, w_ref, o_ref):
    a = a_ref[...]
    an = (a * _rms_scale(a) * ga_ref[...]).astype(BF16)
    o = jnp.dot(an, w_ref[0:ATTN_WIDTH, :], preferred_element_type=F32)
    o = o + jnp.dot(y_ref[...].astype(BF16), w_ref[ATTN_WIDTH:, :], preferred_element_type=F32)
    o_ref[...] = x_ref[...] + o * _rms_scale(o) * gp_ref[...]


def _out_proj(attn, y, x, g_attn, g_post, w, layer, *, tm=512):
    T, D = x.shape
    vec = lambda width: pl.BlockSpec((None, 1, width), lambda i: (layer, 0, 0))
    return pl.pallas_call(
        _out_proj_kernel,
        grid=(T // tm,),
        in_specs=[
            pl.BlockSpec((tm, ATTN_WIDTH), lambda i: (i, 0)),
            pl.BlockSpec((tm, SSD_WIDTH), lambda i: (i, 0)),
            pl.BlockSpec((tm, D), lambda i: (i, 0)),
            vec(ATTN_WIDTH), vec(D),
            pl.BlockSpec((None, ATTN_WIDTH + SSD_WIDTH, D), lambda i: (layer, 0, 0)),
        ],
        out_specs=pl.BlockSpec((tm, D), lambda i: (i, 0)),
        out_shape=jax.ShapeDtypeStruct((T, D), F32),
        compiler_params=pltpu.CompilerParams(
            dimension_semantics=("parallel",), vmem_limit_bytes=VMEM_LIMIT),
        name="out_proj",
    )(attn, y, x, g_attn, g_post, w)


def _mlp_kernel(x_ref, g1_ref, g2_ref, wu_ref, wd_ref, o_ref, h_ref, acc_ref):
    f = pl.program_id(1)

    @pl.when(f == 0)
    def _():
        x = x_ref[...]
        h_ref[...] = ((x * _rms_scale(x)) * g1_ref[...]).astype(BF16)
        acc_ref[...] = jnp.zeros(acc_ref.shape, F32)

    u = jnp.maximum(jnp.dot(h_ref[...], wu_ref[...], preferred_element_type=F32), 0.0)
    acc_ref[...] += jnp.dot((u * u).astype(BF16), wd_ref[...], preferred_element_type=F32)

    @pl.when(f == pl.num_programs(1) - 1)
    def _():
        acc = acc_ref[...]
        o_ref[...] = x_ref[...] + acc * _rms_scale(acc) * g2_ref[...]


def _mlp(x, g_pre, g_post, w_up, w_down, layer, *, tm=512, tf=1024):
    T, D = x.shape
    F = w_up.shape[-1]
    vec = pl.BlockSpec((None, 1, D), lambda i, f: (layer, 0, 0))
    return pl.pallas_call(
        _mlp_kernel,
        grid=(T // tm, F // tf),
        in_specs=[
            pl.BlockSpec((tm, D), lambda i, f: (i, 0)),
            vec, vec,
            pl.BlockSpec((None, D, tf), lambda i, f: (layer, 0, f)),
            pl.BlockSpec((None, tf, D), lambda i, f: (layer, f, 0)),
        ],
        out_specs=pl.BlockSpec((tm, D), lambda i, f: (i, 0)),
        out_shape=jax.ShapeDtypeStruct((T, D), F32),
        scratch_shapes=[pltpu.VMEM((tm, D), BF16), pltpu.VMEM((tm, D), F32)],
        compiler_params=pltpu.CompilerParams(
            dimension_semantics=("parallel", "arbitrary"), vmem_limit_bytes=VMEM_LIMIT),
        name="mlp",
    )(x, g_pre, g_post, w_up, w_down)


def _row(p):
    return p[:, None, :]


def _pad_lanes(p):
    return jnp.pad(p, ((0, 0), (0, LANES - p.shape[-1])))


def kernel(x, positions, pre_mix_norm, post_mix_norm, pre_mlp_norm, post_mlp_norm, w_in, conv_w, conv_b, dt_bias, a_log, d_skip, attn_out_norm, ssd_out_norm, w_out, w_up, w_down):
    B, S, D = x.shape
    depth = w_in.shape[0]
    T = B * S

    w_in_b = lax.dynamic_update_slice(jnp.zeros((depth, D, IN_PAD), BF16), w_in.astype(BF16), (0, 0, 0))
    w_out_b = w_out.astype(BF16)
    w_up_b = w_up.astype(BF16)
    w_down_b = w_down.astype(BF16)
    dt_bias_p = _row(_pad_lanes(dt_bias))
    a_log_p = _row(_pad_lanes(a_log))
    d_skip_x = _row(jnp.repeat(d_skip, SSD_HEAD_DIM, axis=-1))
    conv_b_r = _row(conv_b)
    g_pre_mix, g_post_mix = _row(pre_mix_norm), _row(post_mix_norm)
    g_pre_mlp, g_post_mlp = _row(pre_mlp_norm), _row(post_mlp_norm)
    g_attn, g_ssd = _row(attn_out_norm), _row(ssd_out_norm)

    c_tab, s_tab = _rope_tables(positions)

    xf = x.reshape(T, D)
    for l in range(depth):
        proj = _norm_matmul(xf, g_pre_mix, w_in_b, l).reshape(B, S, IN_PAD)
        attn = _moba_attention(proj, c_tab, s_tab)
        y = _ssd(proj, conv_w, conv_b_r, dt_bias_p, a_log_p, d_skip_x, g_ssd, l)
        xf = _out_proj(attn.reshape(T, ATTN_WIDTH), y.reshape(T, SSD_WIDTH), xf,
                       g_attn, g_post_mix, w_out_b, l)
        xf = _mlp(xf, g_pre_mlp, g_post_mlp, w_up_b, w_down_b, l)
    return xf.reshape(B, S, D)
```

```python
import functools
import math

import jax
import jax.numpy as jnp
from jax import lax
from jax.experimental import pallas as pl
from jax.experimental.pallas import tpu as pltpu

F32 = jnp.float32
BF16 = jnp.bfloat16

D_MODEL = 2048
ATTN_WIDTH = 1024
HEAD_DIM = 128
ATTN_HEADS = ATTN_WIDTH // HEAD_DIM
ROPE_DIM = HEAD_DIM // 4
ROPE_HALF = ROPE_DIM // 2
ROPE_THETA = 500000.0
MOBA_BLOCK = 256
MOBA_TOPK = 3
MOBA_HEADS_PER_STEP = 4

SSD_WIDTH = 1024
SSD_HEAD_DIM = 64
SSD_HEADS = SSD_WIDTH // SSD_HEAD_DIM
SSD_GROUPS = 2
SSD_STATE = 128
SSD_CONV = 4
SSD_CHUNK = 128
SSD_CHUNKS_PER_STEP = 2
SSD_BC = 2 * SSD_GROUPS * SSD_STATE
CONV_DIM = SSD_WIDTH + SSD_BC
IN_DIM = 3 * ATTN_WIDTH + SSD_WIDTH + CONV_DIM + SSD_HEADS
LANES = 128
SUBLANES = 8
IN_PAD = -(-IN_DIM // LANES) * LANES
D_FF = 4 * D_MODEL
NORM_EPS = 1e-6

OFF_Z = 3 * ATTN_WIDTH
OFF_X = OFF_Z + SSD_WIDTH
OFF_BC = OFF_X + SSD_WIDTH
OFF_DT = OFF_BC + SSD_BC

VMEM_LIMIT = 56 * 1024 * 1024

NT_DIMS = (((1,), (1,)), ((), ()))
LOG2E = math.log2(math.e)


def _rms_scale(v):
    return lax.rsqrt(jnp.mean(v * v, axis=-1, keepdims=True) + NORM_EPS)


def _silu(v):
    return v * (0.5 + 0.5 * jnp.tanh(0.5 * v))


def _rope_table_kernel(pos_ref, c_ref, s_ref):
    pos = pos_ref[...].astype(F32)
    lane = lax.broadcasted_iota(jnp.int32, (1, LANES), 1)
    k = (lane & (ROPE_HALF - 1)).astype(F32)
    inv_freq = jnp.exp(k * (-math.log(ROPE_THETA) / ROPE_HALF))
    ang = pos * inv_freq
    cos = jnp.cos(ang)
    sin = jnp.sin(ang)
    in_rope = lane < ROPE_DIM
    c_ref[...] = jnp.where(in_rope, cos, 1.0)
    s_ref[...] = jnp.where(lane < ROPE_HALF, -sin, jnp.where(in_rope, sin, 0.0))


def _rope_tables(positions):
    B, S = positions.shape
    spec = pl.BlockSpec((None, S, LANES), lambda b: (b, 0, 0))
    shape = jax.ShapeDtypeStruct((B, S, LANES), F32)
    return pl.pallas_call(
        _rope_table_kernel,
        grid=(B,),
        in_specs=[pl.BlockSpec((None, S, 1), lambda b: (b, 0, 0))],
        out_specs=[spec, spec],
        out_shape=[shape, shape],
        name="rope_tables",
    )(positions.reshape(B, S, 1))


def _norm_matmul_kernel(x_ref, g_ref, w_ref, o_ref, h_ref):
    @pl.when(pl.program_id(1) == 0)
    def _():
        x = x_ref[...]
        h_ref[...] = ((x * _rms_scale(x)) * g_ref[...]).astype(BF16)

    o_ref[...] = jnp.dot(h_ref[...], w_ref[...], preferred_element_type=F32)


def _norm_matmul(x, gain, w, layer, *, tm=1024, tn=1920):
    T, D = x.shape
    N = w.shape[-1]
    return pl.pallas_call(
        _norm_matmul_kernel,
        grid=(T // tm, N // tn),
        in_specs=[
            pl.BlockSpec((tm, D), lambda i, j: (i, 0)),
            pl.BlockSpec((None, 1, D), lambda i, j: (layer, 0, 0)),
            pl.BlockSpec((None, D, tn), lambda i, j: (layer, 0, j)),
        ],
        out_specs=pl.BlockSpec((tm, tn), lambda i, j: (i, j)),
        out_shape=jax.ShapeDtypeStruct((T, N), F32),
        scratch_shapes=[pltpu.VMEM((tm, D), BF16)],
        compiler_params=pltpu.CompilerParams(
            dimension_semantics=("parallel", "arbitrary"), vmem_limit_bytes=VMEM_LIMIT),
        name="norm_in_proj",
    )(x, gain, w)


def _split_bf16(t):
    hi = t.astype(BF16)
    return hi, (t - hi.astype(F32)).astype(BF16)


def _moba_kernel(q_ref, k_ref, v_ref, c_ref, s_ref, o_ref,
                 kr_ref, vt_ref, qa_ref, bias_ref, *, n_blocks):
    i = pl.program_id(2)
    blk = MOBA_BLOCK

    def swap_halves(t):
        src = lax.broadcasted_iota(jnp.int32, (LANES, LANES), 0)
        dst = lax.broadcasted_iota(jnp.int32, (LANES, LANES), 1)
        swap = jnp.logical_or(jnp.logical_and(dst < ROPE_HALF, src == dst + ROPE_HALF),
                              jnp.logical_and(jnp.logical_and(dst >= ROPE_HALF, dst < ROPE_DIM),
                                              src == dst - ROPE_HALF)).astype(BF16)
        return jnp.dot(jnp.concatenate(_split_bf16(t), axis=1), jnp.concatenate([swap, swap], axis=0),
                       preferred_element_type=F32)

    def rope(t, swapped, rows):
        return t * c_ref[rows, :] + swapped * s_ref[rows, :]

    heads = range(kr_ref.shape[0])
    cols = lambda hh: slice(hh * HEAD_DIM, (hh + 1) * HEAD_DIM)

    @pl.when(i == 0)
    def _():
        for hh in heads:
            k_means = []
            k_swapped = swap_halves(k_ref[:, cols(hh)])
            q_swapped = swap_halves(q_ref[:, cols(hh)])
            for b in range(n_blocks):
                rows = pl.ds(b * blk, blk)
                kb = rope(k_ref[rows, cols(hh)], k_swapped[b * blk:(b + 1) * blk, :], rows)
                kr_ref[hh, rows, :] = kb.astype(BF16)
                k_means.append(jnp.mean(kb, axis=0, keepdims=True))
                vt_ref[hh, :, b * blk:(b + 1) * blk] = v_ref[rows, cols(hh)].astype(BF16).T
            km_hi, km_lo = _split_bf16(jnp.concatenate(k_means, axis=0))
            km_cat = jnp.concatenate([km_hi, km_lo, km_hi], axis=1)
            for qb in range(n_blocks):
                rows = pl.ds(qb * blk, blk)
                q = rope(q_ref[rows, cols(hh)], q_swapped[qb * blk:(qb + 1) * blk, :], rows) * (HEAD_DIM ** -0.5)
                q_hi, q_lo = _split_bf16(q)
                gate = lax.dot_general(km_cat, jnp.concatenate([q_hi, q_hi, q_lo], axis=1),
                                       NT_DIMS, preferred_element_type=F32)
                jidx = lax.broadcasted_iota(jnp.int32, gate.shape, 0)
                past = jidx < qb
                gate = jnp.where(past, gate, -jnp.inf)
                rank = jnp.zeros(gate.shape, jnp.int32)
                for m in range(n_blocks):
                    gm = gate[m:m + 1, :]
                    beats = jnp.logical_or(gm > gate, jnp.logical_and(gm == gate, m < jidx))
                    rank = rank + beats.astype(jnp.int32)
                selected = jnp.logical_and(past, rank < MOBA_TOPK)
                bias_ref[hh, qb] = jnp.where(selected, 0.0, -jnp.inf)
                qa_ref[hh, qb] = (q * LOG2E).astype(BF16)

    q_atts = [qa_ref[hh, i] for hh in heads]
    biases = [bias_ref[hh, i] for hh in heads]

    key_i = lax.broadcasted_iota(jnp.int32, (blk, blk), 0)
    qry_i = lax.broadcasted_iota(jnp.int32, (blk, blk), 1)
    causal_bias = jnp.where(key_i <= qry_i, 0.0, -jnp.inf)

    def attend(n):
        nk = (n + 1) * blk
        for hh in heads:
            bias = biases[hh]
            s = lax.dot_general(kr_ref[hh, 0:nk, :], q_atts[hh], NT_DIMS,
                                preferred_element_type=F32)
            parts = [s[j * blk:(j + 1) * blk, :] + bias[j:j + 1, :] for j in range(n)]
            parts.append(s[n * blk:nk, :] + causal_bias)
            m_max = functools.reduce(jnp.maximum, [jnp.max(t, axis=0, keepdims=True) for t in parts])
            probs = [jnp.exp2(t - m_max) for t in parts]
            denom = functools.reduce(jnp.add, [jnp.sum(t, axis=0, keepdims=True) for t in probs])
            acc = jnp.dot(vt_ref[hh, :, 0:nk], jnp.concatenate(probs, axis=0).astype(BF16),
                          preferred_element_type=F32)
            o_ref[:, cols(hh)] = (acc * (1.0 / denom)).T

    for n in range(n_blocks):
        pl.when(i == n)(functools.partial(attend, n))


def _moba_attention(proj, c_tab, s_tab):
    B, S, _ = proj.shape
    nb = S // MOBA_BLOCK
    hp = MOBA_HEADS_PER_STEP
    G = ATTN_HEADS // hp
    width = hp * HEAD_DIM
    full = lambda off: pl.BlockSpec((None, S, width), lambda b, h, i: (b, 0, off + h))
    tab = pl.BlockSpec((None, S, LANES), lambda b, h, i: (b, 0, 0))
    return pl.pallas_call(
        functools.partial(_moba_kernel, n_blocks=nb),
        grid=(B, G, nb),
        in_specs=[full(0), full(G), full(2 * G), tab, tab],
        out_specs=pl.BlockSpec((None, MOBA_BLOCK, width), lambda b, h, i: (b, i, h)),
        out_shape=jax.ShapeDtypeStruct((B, S, ATTN_WIDTH), F32),
        scratch_shapes=[
            pltpu.VMEM((hp, S, HEAD_DIM), BF16),
            pltpu.VMEM((hp, HEAD_DIM, S), BF16),
            pltpu.VMEM((hp, nb, MOBA_BLOCK, HEAD_DIM), BF16),
            pltpu.VMEM((hp, nb, nb, MOBA_BLOCK), F32),
        ],
        compiler_params=pltpu.CompilerParams(
            dimension_semantics=("parallel", "parallel", "arbitrary"), vmem_limit_bytes=VMEM_LIMIT),
        name="moba_attention",
    )(proj, proj, proj, c_tab, s_tab)


def _ssd_kernel(z_ref, xs_ref, bc_ref, dt_ref, wx_ref, bx_ref, wbc_ref, bbc_ref,
                dtb_ref, alog_ref, dskip_ref, gn_ref, o_ref,
                extx_ref, extbc_ref, state_ref):
    L = SSD_CHUNK
    N = SSD_STATE
    halo = SUBLANES

    @pl.when(pl.program_id(1) == 0)
    def _():
        extx_ref[0:halo, :] = jnp.zeros((halo, SSD_WIDTH), F32)
        extbc_ref[0:halo, :] = jnp.zeros((halo, SSD_BC), F32)
        state_ref[...] = jnp.zeros(state_ref.shape, F32)

    def chunk(rows):
        def conv_silu(ext_ref, cur_ref, w_ref, b_ref):
            ext_ref[halo:halo + L, :] = cur_ref[rows, :]
            acc = b_ref[...] + ext_ref[pl.ds(halo, L), :] * w_ref[SSD_CONV - 1:SSD_CONV, :]
            for k in range(SSD_CONV - 1):
                acc = acc + ext_ref[pl.ds(halo - (SSD_CONV - 1) + k, L), :] * w_ref[k:k + 1, :]
            ext_ref[0:halo, :] = ext_ref[L:L + halo, :]
            return _silu(acc)

        xc = conv_silu(extx_ref, xs_ref, wx_ref, bx_ref)
        bcc = conv_silu(extbc_ref, bc_ref, wbc_ref, bbc_ref)

        dt_in = dt_ref[rows, :] + dtb_ref[...]
        dt = jnp.maximum(dt_in, 0.0) + jnp.log(1.0 + jnp.exp(-jnp.abs(dt_in)))
        a_dt = dt * (-jnp.exp(alog_ref[...]))
        row = lax.broadcasted_iota(jnp.int32, (L, L), 0)
        col = lax.broadcasted_iota(jnp.int32, (L, L), 1)
        tri = row >= col
        a_cs = jnp.dot(tri.astype(F32), a_dt, precision=lax.Precision.HIGHEST,
                       preferred_element_type=F32)
        a_cs_t = a_cs.T[0:SSD_HEADS, :]
        dt_t = dt.T[0:SSD_HEADS, :]
        acs_mlogdt_t = a_cs_t - jnp.log(dt_t)
        a_last = a_cs_t[:, L - 1:L]
        w_t = dt_t * jnp.exp(a_last - a_cs_t)
        chunk_decay = jnp.broadcast_to(jnp.exp(a_last), (SSD_HEADS, LANES))

        lane = lax.broadcasted_iota(jnp.int32, (1, LANES), 1)
        lo = lane < SSD_HEAD_DIM

        def block_diag(v):
            return jnp.concatenate([jnp.where(lo, v, 0.0), jnp.where(lo, 0.0, v)], axis=0).astype(BF16)

        heads_per_group = SSD_HEADS // SSD_GROUPS
        y_parts = []
        for g in range(SSD_GROUPS):
            b_g = bcc[:, g * N:(g + 1) * N]
            c_g = bcc[:, (SSD_GROUPS + g) * N:(SSD_GROUPS + g + 1) * N]
            cb = lax.dot_general(c_g.astype(BF16), b_g.astype(BF16), NT_DIMS,
                                 preferred_element_type=F32)
            b_t = b_g.T
            for pr in range(heads_per_group // 2):
                m_parts, cs_parts, bw_parts = [], [], []
                for e in (g * heads_per_group + 2 * pr, g * heads_per_group + 2 * pr + 1):
                    colb = jnp.broadcast_to(a_cs[:, e:e + 1], (L, L))
                    decay_dt = jnp.where(tri, jnp.exp(colb - acs_mlogdt_t[e:e + 1, :]), 0.0)
                    m_parts.append((cb * decay_dt).astype(BF16))
                    cs_parts.append((c_g * jnp.exp(colb)).astype(BF16))
                    bw_parts.append((b_t * w_t[e:e + 1, :]).astype(BF16))
                pair = g * (heads_per_group // 2) + pr
                e0 = 2 * pair
                x_bd = block_diag(xc[:, pair * LANES:(pair + 1) * LANES])
                st = state_ref[pair]
                rhs = jnp.concatenate([x_bd, block_diag(st)], axis=0)
                y_parts.append(jnp.dot(jnp.concatenate(m_parts + cs_parts, axis=1), rhs,
                                       preferred_element_type=F32))
                st_new = jnp.dot(jnp.concatenate(bw_parts, axis=1), x_bd,
                                 preferred_element_type=F32)
                dec = jnp.where(lo, chunk_decay[e0:e0 + 1, :], chunk_decay[e0 + 1:e0 + 2, :])
                state_ref[pair] = st * dec + st_new

        y = jnp.concatenate(y_parts, axis=1) + dskip_ref[...] * xc
        y = y * _silu(z_ref[rows, :])
        gw = SSD_WIDTH // SSD_GROUPS
        outs = []
        for g in range(SSD_GROUPS):
            yg = y[:, g * gw:(g + 1) * gw]
            outs.append(yg * _rms_scale(yg) * gn_ref[:, g * gw:(g + 1) * gw])
        o_ref[rows, :] = jnp.concatenate(outs, axis=1)

    for sub in range(o_ref.shape[0] // L):
        chunk(pl.ds(sub * L, L))


def _ssd(proj, conv_w, conv_b, dt_bias, a_log, d_skip, gn, layer):
    B, S, _ = proj.shape
    L = SSD_CHUNKS_PER_STEP * SSD_CHUNK
    col = lambda width, off: pl.BlockSpec((None, L, width), lambda b, c: (b, c, off // width))
    par = lambda rows, width, off: pl.BlockSpec((None, rows, width), lambda b, c: (layer, 0, off // width))
    return pl.pallas_call(
        _ssd_kernel,
        grid=(B, S // L),
        in_specs=[
            col(SSD_WIDTH, OFF_Z), col(SSD_WIDTH, OFF_X), col(SSD_BC, OFF_BC), col(LANES, OFF_DT),
            par(SSD_CONV, SSD_WIDTH, 0), par(1, SSD_WIDTH, 0),
            par(SSD_CONV, SSD_BC, SSD_WIDTH), par(1, SSD_BC, SSD_WIDTH),
            par(1, LANES, 0), par(1, LANES, 0), par(1, SSD_WIDTH, 0), par(1, SSD_WIDTH, 0),
        ],
        out_specs=pl.BlockSpec((None, L, SSD_WIDTH), lambda b, c: (b, c, 0)),
        out_shape=jax.ShapeDtypeStruct((B, S, SSD_WIDTH), F32),
        scratch_shapes=[
            pltpu.VMEM((SSD_CHUNK + SUBLANES, SSD_WIDTH), F32),
            pltpu.VMEM((SSD_CHUNK + SUBLANES, SSD_BC), F32),
            pltpu.VMEM((SSD_HEADS // 2, SSD_STATE, LANES), F32),
        ],
        compiler_params=pltpu.CompilerParams(
            dimension_semantics=("parallel", "arbitrary"), vmem_limit_bytes=VMEM_LIMIT),
        name="ssd_scan",
    )(proj, proj, proj, proj, conv_w, conv_b, conv_w, conv_b, dt_bias, a_log, d_skip, gn)


def _out_proj_kernel(a_ref, y_ref, x_ref, ga_ref, gp_ref, wf_ref, o_ref, w_ref):
    @pl.when(pl.program_id(0) == 0)
    def _():
        w_ref[...] = wf_ref[...].astype(BF16)

    a = a_ref[...]
    an = (a * _rms_scale(a) * ga_ref[...]).astype(BF16)
    o = jnp.dot(an, w_ref[0:ATTN_WIDTH, :], preferred_element_type=F32)
    o = o + jnp.dot(y_ref[...].astype(BF16), w_ref[ATTN_WIDTH:, :], preferred_element_type=F32)
    o_ref[...] = x_ref[...] + o * _rms_scale(o) * gp_ref[...]


def _out_proj(attn, y, x, g_attn, g_post, w, layer, *, tm=512):
    T, D = x.shape
    vec = lambda width: pl.BlockSpec((None, 1, width), lambda i: (layer, 0, 0))
    return pl.pallas_call(
        _out_proj_kernel,
        grid=(T // tm,),
        in_specs=[
            pl.BlockSpec((tm, ATTN_WIDTH), lambda i: (i, 0)),
            pl.BlockSpec((tm, SSD_WIDTH), lambda i: (i, 0)),
            pl.BlockSpec((tm, D), lambda i: (i, 0)),
            vec(ATTN_WIDTH), vec(D),
            pl.BlockSpec((None, ATTN_WIDTH + SSD_WIDTH, D), lambda i: (layer, 0, 0),
                         pipeline_mode=pl.Buffered(1)),
        ],
        out_specs=pl.BlockSpec((tm, D), lambda i: (i, 0)),
        out_shape=jax.ShapeDtypeStruct((T, D), F32),
        scratch_shapes=[pltpu.VMEM((ATTN_WIDTH + SSD_WIDTH, D), BF16)],
        compiler_params=pltpu.CompilerParams(
            dimension_semantics=("arbitrary",), vmem_limit_bytes=VMEM_LIMIT),
        name="out_proj",
    )(attn, y, x, g_attn, g_post, w)


def _mlp_kernel(x_ref, g1_ref, g2_ref, wu_ref, wd_ref, o_ref, h_ref, acc_ref):
    f = pl.program_id(1)

    @pl.when(f == 0)
    def _():
        x = x_ref[...]
        h_ref[...] = ((x * _rms_scale(x)) * g1_ref[...]).astype(BF16)
        acc_ref[...] = jnp.zeros(acc_ref.shape, F32)

    u = jnp.maximum(jnp.dot(h_ref[...], wu_ref[...], preferred_element_type=F32), 0.0)
    acc_ref[...] += jnp.dot((u * u).astype(BF16), wd_ref[...], preferred_element_type=F32)

    @pl.when(f == pl.num_programs(1) - 1)
    def _():
        acc = acc_ref[...]
        o_ref[...] = x_ref[...] + acc * _rms_scale(acc) * g2_ref[...]


def _mlp(x, g_pre, g_post, w_up, w_down, layer, *, tm=512, tf=1024):
    T, D = x.shape
    F = w_up.shape[-1]
    vec = pl.BlockSpec((None, 1, D), lambda i, f: (layer, 0, 0))
    return pl.pallas_call(
        _mlp_kernel,
        grid=(T // tm, F // tf),
        in_specs=[
            pl.BlockSpec((tm, D), lambda i, f: (i, 0)),
            vec, vec,
            pl.BlockSpec((None, D, tf), lambda i, f: (layer, 0, f)),
            pl.BlockSpec((None, tf, D), lambda i, f: (layer, f, 0)),
        ],
        out_specs=pl.BlockSpec((tm, D), lambda i, f: (i, 0)),
        out_shape=jax.ShapeDtypeStruct((T, D), F32),
        scratch_shapes=[pltpu.VMEM((tm, D), BF16), pltpu.VMEM((tm, D), F32)],
        compiler_params=pltpu.CompilerParams(
            dimension_semantics=("parallel", "arbitrary"), vmem_limit_bytes=VMEM_LIMIT),
        name="mlp",
    )(x, g_pre, g_post, w_up, w_down)


def _row(p):
    return p[:, None, :]


def _pad_lanes(p):
    return jnp.pad(p, ((0, 0), (0, LANES - p.shape[-1])))


def kernel(x, positions, pre_mix_norm, post_mix_norm, pre_mlp_norm, post_mlp_norm, w_in, conv_w, conv_b, dt_bias, a_log, d_skip, attn_out_norm, ssd_out_norm, w_out, w_up, w_down):
    B, S, D = x.shape
    depth = w_in.shape[0]
    T = B * S

    w_in_b = lax.dynamic_update_slice(jnp.zeros((depth, D, IN_PAD), BF16), w_in.astype(BF16), (0, 0, 0))
    w_up_b = w_up.astype(BF16)
    w_down_b = w_down.astype(BF16)
    dt_bias_p = _row(_pad_lanes(dt_bias))
    a_log_p = _row(_pad_lanes(a_log))
    d_skip_x = _row(jnp.repeat(d_skip, SSD_HEAD_DIM, axis=-1))
    conv_b_r = _row(conv_b)
    g_pre_mix, g_post_mix = _row(pre_mix_norm), _row(post_mix_norm)
    g_pre_mlp, g_post_mlp = _row(pre_mlp_norm), _row(post_mlp_norm)
    g_attn, g_ssd = _row(attn_out_norm), _row(ssd_out_norm)

    c_tab, s_tab = _rope_tables(positions)

    xf = x.reshape(T, D)
    for l in range(depth):
        proj = _norm_matmul(xf, g_pre_mix, w_in_b, l).reshape(B, S, IN_PAD)
        attn = _moba_attention(proj, c_tab, s_tab)
        y = _ssd(proj, conv_w, conv_b_r, dt_bias_p, a_log_p, d_skip_x, g_ssd, l)
        xf = _out_proj(attn.reshape(T, ATTN_WIDTH), y.reshape(T, SSD_WIDTH), xf,
                       g_attn, g_post_mix, w_out, l)
        xf = _mlp(xf, g_pre_mlp, g_post_mlp, w_up_b, w_down_b, l)
    return xf.reshape(B, S, D)
```

```python
import functools
import math

import jax
import jax.numpy as jnp
from jax import lax
from jax.experimental import pallas as pl
from jax.experimental.pallas import tpu as pltpu

F32 = jnp.float32
BF16 = jnp.bfloat16

D_MODEL = 2048
ATTN_WIDTH = 1024
HEAD_DIM = 128
ATTN_HEADS = ATTN_WIDTH // HEAD_DIM
ROPE_DIM = HEAD_DIM // 4
ROPE_HALF = ROPE_DIM // 2
ROPE_THETA = 500000.0
MOBA_BLOCK = 256
MOBA_TOPK = 3
MOBA_HEADS_PER_STEP = 4

SSD_WIDTH = 1024
SSD_HEAD_DIM = 64
SSD_HEADS = SSD_WIDTH // SSD_HEAD_DIM
SSD_GROUPS = 2
SSD_STATE = 128
SSD_CONV = 4
SSD_CHUNK = 128
SSD_CHUNKS_PER_STEP = 2
SSD_BC = 2 * SSD_GROUPS * SSD_STATE
CONV_DIM = SSD_WIDTH + SSD_BC
IN_DIM = 3 * ATTN_WIDTH + SSD_WIDTH + CONV_DIM + SSD_HEADS
LANES = 128
SUBLANES = 8
IN_PAD = -(-IN_DIM // LANES) * LANES
D_FF = 4 * D_MODEL
NORM_EPS = 1e-6

OFF_Z = 3 * ATTN_WIDTH
OFF_X = OFF_Z + SSD_WIDTH
OFF_BC = OFF_X + SSD_WIDTH
OFF_DT = OFF_BC + SSD_BC

VMEM_LIMIT = 56 * 1024 * 1024

NT_DIMS = (((1,), (1,)), ((), ()))
LOG2E = math.log2(math.e)


def _rms_scale(v):
    return lax.rsqrt(jnp.mean(v * v, axis=-1, keepdims=True) + NORM_EPS)


def _silu(v):
    return v * (0.5 + 0.5 * jnp.tanh(0.5 * v))


def _rope_table_kernel(pos_ref, c_ref, s_ref):
    pos = pos_ref[...].astype(F32)
    lane = lax.broadcasted_iota(jnp.int32, (1, LANES), 1)
    k = (lane & (ROPE_HALF - 1)).astype(F32)
    inv_freq = jnp.exp(k * (-math.log(ROPE_THETA) / ROPE_HALF))
    ang = pos * inv_freq
    cos = jnp.cos(ang)
    sin = jnp.sin(ang)
    in_rope = lane < ROPE_DIM
    c_ref[...] = jnp.where(in_rope, cos, 1.0)
    s_ref[...] = jnp.where(lane < ROPE_HALF, -sin, jnp.where(in_rope, sin, 0.0))


def _rope_tables(positions):
    B, S = positions.shape
    spec = pl.BlockSpec((None, S, LANES), lambda b: (b, 0, 0))
    shape = jax.ShapeDtypeStruct((B, S, LANES), F32)
    return pl.pallas_call(
        _rope_table_kernel,
        grid=(B,),
        in_specs=[pl.BlockSpec((None, S, 1), lambda b: (b, 0, 0))],
        out_specs=[spec, spec],
        out_shape=[shape, shape],
        name="rope_tables",
    )(positions.reshape(B, S, 1))


def _norm_matmul_kernel(x_ref, g_ref, w_ref, o_ref, h_ref):
    @pl.when(pl.program_id(1) == 0)
    def _():
        x = x_ref[...]
        h_ref[...] = ((x * _rms_scale(x)) * g_ref[...]).astype(BF16)

    o_ref[...] = jnp.dot(h_ref[...], w_ref[...], preferred_element_type=F32)


def _norm_matmul(x, gain, w, layer, *, tm=1024, tn=1920):
    T, D = x.shape
    N = w.shape[-1]
    return pl.pallas_call(
        _norm_matmul_kernel,
        grid=(T // tm, N // tn),
        in_specs=[
            pl.BlockSpec((tm, D), lambda i, j: (i, 0)),
            pl.BlockSpec((None, 1, D), lambda i, j: (layer, 0, 0)),
            pl.BlockSpec((None, D, tn), lambda i, j: (layer, 0, j)),
        ],
        out_specs=pl.BlockSpec((tm, tn), lambda i, j: (i, j)),
        out_shape=jax.ShapeDtypeStruct((T, N), F32),
        scratch_shapes=[pltpu.VMEM((tm, D), BF16)],
        compiler_params=pltpu.CompilerParams(
            dimension_semantics=("parallel", "arbitrary"), vmem_limit_bytes=VMEM_LIMIT),
        name="norm_in_proj",
    )(x, gain, w)


def _split_bf16(t):
    hi = t.astype(BF16)
    return hi, (t - hi.astype(F32)).astype(BF16)


def _moba_kernel(q_ref, k_ref, v_ref, c_ref, s_ref, o_ref,
                 kr_ref, vt_ref, qa_ref, bias_ref, *, n_blocks):
    i = pl.program_id(2)
    blk = MOBA_BLOCK

    def swap_halves(t):
        src = lax.broadcasted_iota(jnp.int32, (LANES, LANES), 0)
        dst = lax.broadcasted_iota(jnp.int32, (LANES, LANES), 1)
        swap = jnp.logical_or(jnp.logical_and(dst < ROPE_HALF, src == dst + ROPE_HALF),
                              jnp.logical_and(jnp.logical_and(dst >= ROPE_HALF, dst < ROPE_DIM),
                                              src == dst - ROPE_HALF)).astype(BF16)
        return jnp.dot(jnp.concatenate(_split_bf16(t), axis=1), jnp.concatenate([swap, swap], axis=0),
                       preferred_element_type=F32)

    def rope(t, swapped, rows):
        return t * c_ref[rows, :] + swapped * s_ref[rows, :]

    heads = range(kr_ref.shape[0])
    cols = lambda hh: slice(hh * HEAD_DIM, (hh + 1) * HEAD_DIM)

    @pl.when(i == 0)
    def _():
        for hh in heads:
            k_means = []
            k_swapped = swap_halves(k_ref[:, cols(hh)])
            q_swapped = swap_halves(q_ref[:, cols(hh)])
            for b in range(n_blocks):
                rows = pl.ds(b * blk, blk)
                kb = rope(k_ref[rows, cols(hh)], k_swapped[b * blk:(b + 1) * blk, :], rows)
                kr_ref[hh, rows, :] = kb.astype(BF16)
                k_means.append(jnp.mean(kb, axis=0, keepdims=True))
                vt_ref[hh, :, b * blk:(b + 1) * blk] = v_ref[rows, cols(hh)].astype(BF16).T
            km_hi, km_lo = _split_bf16(jnp.concatenate(k_means, axis=0))
            km_cat = jnp.concatenate([km_hi, km_lo, km_hi], axis=1)
            for qb in range(n_blocks):
                rows = pl.ds(qb * blk, blk)
                q = rope(q_ref[rows, cols(hh)], q_swapped[qb * blk:(qb + 1) * blk, :], rows) * (HEAD_DIM ** -0.5)
                q_hi, q_lo = _split_bf16(q)
                gate = lax.dot_general(km_cat, jnp.concatenate([q_hi, q_hi, q_lo], axis=1),
                                       NT_DIMS, preferred_element_type=F32)
                jidx = lax.broadcasted_iota(jnp.int32, gate.shape, 0)
                past = jidx < qb
                gate = jnp.where(past, gate, -jnp.inf)
                rank = jnp.zeros(gate.shape, jnp.int32)
                for m in range(n_blocks):
                    gm = gate[m:m + 1, :]
                    beats = jnp.logical_or(gm > gate, jnp.logical_and(gm == gate, m < jidx))
                    rank = rank + beats.astype(jnp.int32)
                selected = jnp.logical_and(past, rank < MOBA_TOPK)
                bias_ref[hh, qb] = jnp.where(selected, 0.0, -jnp.inf)
                qa_ref[hh, qb] = (q * LOG2E).astype(BF16)

    q_atts = [qa_ref[hh, i] for hh in heads]
    biases = [bias_ref[hh, i] for hh in heads]

    key_i = lax.broadcasted_iota(jnp.int32, (blk, blk), 0)
    qry_i = lax.broadcasted_iota(jnp.int32, (blk, blk), 1)
    causal_bias = jnp.where(key_i <= qry_i, 0.0, -jnp.inf)

    def attend(n):
        nk = (n + 1) * blk
        for hh in heads:
            bias = biases[hh]
            s = lax.dot_general(kr_ref[hh, 0:nk, :], q_atts[hh], NT_DIMS,
                                preferred_element_type=F32)
            parts = [s[j * blk:(j + 1) * blk, :] + bias[j:j + 1, :] for j in range(n)]
            parts.append(s[n * blk:nk, :] + causal_bias)
            m_max = functools.reduce(jnp.maximum, [jnp.max(t, axis=0, keepdims=True) for t in parts])
            probs = [jnp.exp2(t - m_max) for t in parts]
            denom = functools.reduce(jnp.add, [jnp.sum(t, axis=0, keepdims=True) for t in probs])
            acc = jnp.dot(vt_ref[hh, :, 0:nk], jnp.concatenate(probs, axis=0).astype(BF16),
                          preferred_element_type=F32)
            o_ref[:, cols(hh)] = (acc * (1.0 / denom)).T

    for n in range(n_blocks):
        pl.when(i == n)(functools.partial(attend, n))


def _moba_attention(proj, c_tab, s_tab):
    B, S, _ = proj.shape
    nb = S // MOBA_BLOCK
    hp = MOBA_HEADS_PER_STEP
    G = ATTN_HEADS // hp
    width = hp * HEAD_DIM
    full = lambda off: pl.BlockSpec((None, S, width), lambda b, h, i: (b, 0, off + h))
    tab = pl.BlockSpec((None, S, LANES), lambda b, h, i: (b, 0, 0))
    return pl.pallas_call(
        functools.partial(_moba_kernel, n_blocks=nb),
        grid=(B, G, nb),
        in_specs=[full(0), full(G), full(2 * G), tab, tab],
        out_specs=pl.BlockSpec((None, MOBA_BLOCK, width), lambda b, h, i: (b, i, h)),
        out_shape=jax.ShapeDtypeStruct((B, S, ATTN_WIDTH), F32),
        scratch_shapes=[
            pltpu.VMEM((hp, S, HEAD_DIM), BF16),
            pltpu.VMEM((hp, HEAD_DIM, S), BF16),
            pltpu.VMEM((hp, nb, MOBA_BLOCK, HEAD_DIM), BF16),
            pltpu.VMEM((hp, nb, nb, MOBA_BLOCK), F32),
        ],
        compiler_params=pltpu.CompilerParams(
            dimension_semantics=("parallel", "parallel", "arbitrary"), vmem_limit_bytes=VMEM_LIMIT),
        name="moba_attention",
    )(proj, proj, proj, c_tab, s_tab)


def _ssd_kernel(z_ref, xs_ref, bc_ref, dt_ref, wx_ref, bx_ref, wbc_ref, bbc_ref,
                dtb_ref, alog_ref, dskip_ref, gn_ref, o_ref,
                extx_ref, extbc_ref, state_ref):
    L = SSD_CHUNK
    N = SSD_STATE
    halo = SUBLANES

    @pl.when(pl.program_id(1) == 0)
    def _():
        extx_ref[0:halo, :] = jnp.zeros((halo, SSD_WIDTH), F32)
        extbc_ref[0:halo, :] = jnp.zeros((halo, SSD_BC), F32)
        state_ref[...] = jnp.zeros(state_ref.shape, F32)

    def chunk(rows):
        def conv_silu(ext_ref, cur_ref, w_ref, b_ref):
            ext_ref[halo:halo + L, :] = cur_ref[rows, :]
            acc = b_ref[...] + ext_ref[pl.ds(halo, L), :] * w_ref[SSD_CONV - 1:SSD_CONV, :]
            for k in range(SSD_CONV - 1):
                acc = acc + ext_ref[pl.ds(halo - (SSD_CONV - 1) + k, L), :] * w_ref[k:k + 1, :]
            ext_ref[0:halo, :] = ext_ref[L:L + halo, :]
            return _silu(acc)

        xc = conv_silu(extx_ref, xs_ref, wx_ref, bx_ref)
        bcc = conv_silu(extbc_ref, bc_ref, wbc_ref, bbc_ref)

        dt_in = dt_ref[rows, :] + dtb_ref[...]
        dt = jnp.maximum(dt_in, 0.0) + jnp.log(1.0 + jnp.exp(-jnp.abs(dt_in)))
        a_dt = dt * (-jnp.exp(alog_ref[...]))
        row = lax.broadcasted_iota(jnp.int32, (L, L), 0)
        col = lax.broadcasted_iota(jnp.int32, (L, L), 1)
        tri = row >= col
        a_cs = jnp.dot(tri.astype(F32), a_dt, precision=lax.Precision.HIGHEST,
                       preferred_element_type=F32)
        a_cs_t = a_cs.T[0:SSD_HEADS, :]
        dt_t = dt.T[0:SSD_HEADS, :]
        acs_mlogdt_t = a_cs_t - jnp.log(dt_t)
        a_last = a_cs_t[:, L - 1:L]
        w_t = dt_t * jnp.exp(a_last - a_cs_t)
        chunk_decay = jnp.broadcast_to(jnp.exp(a_last), (SSD_HEADS, LANES))

        lane = lax.broadcasted_iota(jnp.int32, (1, LANES), 1)
        lo = lane < SSD_HEAD_DIM

        def block_diag(v):
            return jnp.concatenate([jnp.where(lo, v, 0.0), jnp.where(lo, 0.0, v)], axis=0).astype(BF16)

        heads_per_group = SSD_HEADS // SSD_GROUPS
        y_parts = []
        for g in range(SSD_GROUPS):
            b_g = bcc[:, g * N:(g + 1) * N]
            c_g = bcc[:, (SSD_GROUPS + g) * N:(SSD_GROUPS + g + 1) * N]
            cb = lax.dot_general(c_g.astype(BF16), b_g.astype(BF16), NT_DIMS,
                                 preferred_element_type=F32)
            b_t = b_g.T
            for pr in range(heads_per_group // 2):
                m_parts, cs_parts, bw_parts = [], [], []
                for e in (g * heads_per_group + 2 * pr, g * heads_per_group + 2 * pr + 1):
                    colb = jnp.broadcast_to(a_cs[:, e:e + 1], (L, L))
                    decay_dt = jnp.where(tri, jnp.exp(colb - acs_mlogdt_t[e:e + 1, :]), 0.0)
                    m_parts.append((cb * decay_dt).astype(BF16))
                    cs_parts.append((c_g * jnp.exp(colb)).astype(BF16))
                    bw_parts.append((b_t * w_t[e:e + 1, :]).astype(BF16))
                pair = g * (heads_per_group // 2) + pr
                e0 = 2 * pair
                x_bd = block_diag(xc[:, pair * LANES:(pair + 1) * LANES])
                st = state_ref[pair]
                rhs = jnp.concatenate([x_bd, block_diag(st)], axis=0)
                y_parts.append(jnp.dot(jnp.concatenate(m_parts + cs_parts, axis=1), rhs,
                                       preferred_element_type=F32))
                st_new = jnp.dot(jnp.concatenate(bw_parts, axis=1), x_bd,
                                 preferred_element_type=F32)
                dec = jnp.where(lo, chunk_decay[e0:e0 + 1, :], chunk_decay[e0 + 1:e0 + 2, :])
                state_ref[pair] = st * dec + st_new

        y = jnp.concatenate(y_parts, axis=1) + dskip_ref[...] * xc
        y = y * _silu(z_ref[rows, :])
        gw = SSD_WIDTH // SSD_GROUPS
        outs = []
        for g in range(SSD_GROUPS):
            yg = y[:, g * gw:(g + 1) * gw]
            outs.append(yg * _rms_scale(yg) * gn_ref[:, g * gw:(g + 1) * gw])
        o_ref[rows, :] = jnp.concatenate(outs, axis=1)

    for sub in range(o_ref.shape[0] // L):
        chunk(pl.ds(sub * L, L))


def _ssd(proj, conv_w, conv_b, dt_bias, a_log, d_skip, gn, layer):
    B, S, _ = proj.shape
    L = SSD_CHUNKS_PER_STEP * SSD_CHUNK
    col = lambda width, off: pl.BlockSpec((None, L, width), lambda b, c: (b, c, off // width))
    par = lambda rows, width, off: pl.BlockSpec((None, rows, width), lambda b, c: (layer, 0, off // width))
    return pl.pallas_call(
        _ssd_kernel,
        grid=(B, S // L),
        in_specs=[
            col(SSD_WIDTH, OFF_Z), col(SSD_WIDTH, OFF_X), col(SSD_BC, OFF_BC), col(LANES, OFF_DT),
            par(SSD_CONV, SSD_WIDTH, 0), par(1, SSD_WIDTH, 0),
            par(SSD_CONV, SSD_BC, SSD_WIDTH), par(1, SSD_BC, SSD_WIDTH),
            par(1, LANES, 0), par(1, LANES, 0), par(1, SSD_WIDTH, 0), par(1, SSD_WIDTH, 0),
        ],
        out_specs=pl.BlockSpec((None, L, SSD_WIDTH), lambda b, c: (b, c, 0)),
        out_shape=jax.ShapeDtypeStruct((B, S, SSD_WIDTH), F32),
        scratch_shapes=[
            pltpu.VMEM((SSD_CHUNK + SUBLANES, SSD_WIDTH), F32),
            pltpu.VMEM((SSD_CHUNK + SUBLANES, SSD_BC), F32),
            pltpu.VMEM((SSD_HEADS // 2, SSD_STATE, LANES), F32),
        ],
        compiler_params=pltpu.CompilerParams(
            dimension_semantics=("parallel", "arbitrary"), vmem_limit_bytes=VMEM_LIMIT),
        name="ssd_scan",
    )(proj, proj, proj, proj, conv_w, conv_b, conv_w, conv_b, dt_bias, a_log, d_skip, gn)


def _out_proj_kernel(a_ref, y_ref, x_ref, ga_ref, gp_ref, wf_ref, o_ref, w_ref):
    @pl.when(pl.program_id(0) == 0)
    def _():
        w_ref[...] = wf_ref[...].astype(BF16)

    a = a_ref[...]
    an = (a * _rms_scale(a) * ga_ref[...]).astype(BF16)
    o = jnp.dot(an, w_ref[0:ATTN_WIDTH, :], preferred_element_type=F32)
    o = o + jnp.dot(y_ref[...].astype(BF16), w_ref[ATTN_WIDTH:, :], preferred_element_type=F32)
    o_ref[...] = x_ref[...] + o * _rms_scale(o) * gp_ref[...]


def _out_proj(attn, y, x, g_attn, g_post, w, layer, *, tm=512):
    T, D = x.shape
    vec = lambda width: pl.BlockSpec((None, 1, width), lambda i: (layer, 0, 0))
    return pl.pallas_call(
        _out_proj_kernel,
        grid=(T // tm,),
        in_specs=[
            pl.BlockSpec((tm, ATTN_WIDTH), lambda i: (i, 0)),
            pl.BlockSpec((tm, SSD_WIDTH), lambda i: (i, 0)),
            pl.BlockSpec((tm, D), lambda i: (i, 0)),
            vec(ATTN_WIDTH), vec(D),
            pl.BlockSpec((None, ATTN_WIDTH + SSD_WIDTH, D), lambda i: (layer, 0, 0),
                         pipeline_mode=pl.Buffered(1)),
        ],
        out_specs=pl.BlockSpec((tm, D), lambda i: (i, 0)),
        out_shape=jax.ShapeDtypeStruct((T, D), F32),
        scratch_shapes=[pltpu.VMEM((ATTN_WIDTH + SSD_WIDTH, D), BF16)],
        compiler_params=pltpu.CompilerParams(
            dimension_semantics=("arbitrary",), vmem_limit_bytes=VMEM_LIMIT),
        name="out_proj",
    )(attn, y, x, g_attn, g_post, w)


def _mlp_kernel(x_ref, g1_ref, g2_ref, wu_ref, wd_ref, o_ref, h_ref, acc_ref):
    f = pl.program_id(1)

    @pl.when(f == 0)
    def _():
        x = x_ref[...]
        h_ref[...] = ((x * _rms_scale(x)) * g1_ref[...]).astype(BF16)
        acc_ref[...] = jnp.zeros(acc_ref.shape, F32)

    u = jnp.maximum(jnp.dot(h_ref[...], wu_ref[...], preferred_element_type=F32), 0.0)
    acc_ref[...] += jnp.dot((u * u).astype(BF16), wd_ref[...], preferred_element_type=F32)

    @pl.when(f == pl.num_programs(1) - 1)
    def _():
        acc = acc_ref[...]
        o_ref[...] = x_ref[...] + acc * _rms_scale(acc) * g2_ref[...]


def _mlp(x, g_pre, g_post, w_up, w_down, layer, *, tm=512, tf=2048):
    T, D = x.shape
    F = w_up.shape[-1]
    vec = pl.BlockSpec((None, 1, D), lambda i, f: (layer, 0, 0))
    return pl.pallas_call(
        _mlp_kernel,
        grid=(T // tm, F // tf),
        in_specs=[
            pl.BlockSpec((tm, D), lambda i, f: (i, 0), pipeline_mode=pl.Buffered(1)),
            vec, vec,
            pl.BlockSpec((None, D, tf), lambda i, f: (layer, 0, f)),
            pl.BlockSpec((None, tf, D), lambda i, f: (layer, f, 0)),
        ],
        out_specs=pl.BlockSpec((tm, D), lambda i, f: (i, 0), pipeline_mode=pl.Buffered(1)),
        out_shape=jax.ShapeDtypeStruct((T, D), F32),
        scratch_shapes=[pltpu.VMEM((tm, D), BF16), pltpu.VMEM((tm, D), F32)],
        compiler_params=pltpu.CompilerParams(
            dimension_semantics=("parallel", "arbitrary"), vmem_limit_bytes=VMEM_LIMIT),
        name="mlp",
    )(x, g_pre, g_post, w_up, w_down)


def _row(p):
    return p[:, None, :]


def _pad_lanes(p):
    return jnp.pad(p, ((0, 0), (0, LANES - p.shape[-1])))


def kernel(x, positions, pre_mix_norm, post_mix_norm, pre_mlp_norm, post_mlp_norm, w_in, conv_w, conv_b, dt_bias, a_log, d_skip, attn_out_norm, ssd_out_norm, w_out, w_up, w_down):
    B, S, D = x.shape
    depth = w_in.shape[0]
    T = B * S

    w_in_b = lax.dynamic_update_slice(jnp.zeros((depth, D, IN_PAD), BF16), w_in.astype(BF16), (0, 0, 0))
    w_up_b = w_up.astype(BF16)
    w_down_b = w_down.astype(BF16)
    dt_bias_p = _row(_pad_lanes(dt_bias))
    a_log_p = _row(_pad_lanes(a_log))
    d_skip_x = _row(jnp.repeat(d_skip, SSD_HEAD_DIM, axis=-1))
    conv_b_r = _row(conv_b)
    g_pre_mix, g_post_mix = _row(pre_mix_norm), _row(post_mix_norm)
    g_pre_mlp, g_post_mlp = _row(pre_mlp_norm), _row(post_mlp_norm)
    g_attn, g_ssd = _row(attn_out_norm), _row(ssd_out_norm)

    c_tab, s_tab = _rope_tables(positions)

    xf = x.reshape(T, D)
    for l in range(depth):
        proj = _norm_matmul(xf, g_pre_mix, w_in_b, l).reshape(B, S, IN_PAD)
        attn = _moba_attention(proj, c_tab, s_tab)
        y = _ssd(proj, conv_w, conv_b_r, dt_bias_p, a_log_p, d_skip_x, g_ssd, l)
        xf = _out_proj(attn.reshape(T, ATTN_WIDTH), y.reshape(T, SSD_WIDTH), xf,
                       g_attn, g_post_mix, w_out, l)
        xf = _mlp(xf, g_pre_mlp, g_post_mlp, w_up_b, w_down_b, l)
    return xf.reshape(B, S, D)
```

```python
import functools
import math

import jax
import jax.numpy as jnp
from jax import lax
from jax.experimental import pallas as pl
from jax.experimental.pallas import tpu as pltpu

F32 = jnp.float32
BF16 = jnp.bfloat16

D_MODEL = 2048
ATTN_WIDTH = 1024
HEAD_DIM = 128
ATTN_HEADS = ATTN_WIDTH // HEAD_DIM
ROPE_DIM = HEAD_DIM // 4
ROPE_HALF = ROPE_DIM // 2
ROPE_THETA = 500000.0
MOBA_BLOCK = 256
MOBA_TOPK = 3
MOBA_HEADS_PER_STEP = 4

SSD_WIDTH = 1024
SSD_HEAD_DIM = 64
SSD_HEADS = SSD_WIDTH // SSD_HEAD_DIM
SSD_GROUPS = 2
SSD_STATE = 128
SSD_CONV = 4
SSD_CHUNK = 128
SSD_CHUNKS_PER_STEP = 4
SSD_BC = 2 * SSD_GROUPS * SSD_STATE
CONV_DIM = SSD_WIDTH + SSD_BC
IN_DIM = 3 * ATTN_WIDTH + SSD_WIDTH + CONV_DIM + SSD_HEADS
LANES = 128
SUBLANES = 8
IN_PAD = -(-IN_DIM // LANES) * LANES
D_FF = 4 * D_MODEL
NORM_EPS = 1e-6

OFF_Z = 3 * ATTN_WIDTH
OFF_X = OFF_Z + SSD_WIDTH
OFF_BC = OFF_X + SSD_WIDTH
OFF_DT = OFF_BC + SSD_BC

VMEM_LIMIT = 56 * 1024 * 1024

NT_DIMS = (((1,), (1,)), ((), ()))
LOG2E = math.log2(math.e)


def _rms_scale(v):
    return lax.rsqrt(jnp.mean(v * v, axis=-1, keepdims=True) + NORM_EPS)


def _silu(v):
    return v * (0.5 + 0.5 * jnp.tanh(0.5 * v))


def _rope_table_kernel(pos_ref, c_ref, s_ref):
    pos = pos_ref[...].astype(F32)
    lane = lax.broadcasted_iota(jnp.int32, (1, LANES), 1)
    k = (lane & (ROPE_HALF - 1)).astype(F32)
    inv_freq = jnp.exp(k * (-math.log(ROPE_THETA) / ROPE_HALF))
    ang = pos * inv_freq
    cos = jnp.cos(ang)
    sin = jnp.sin(ang)
    in_rope = lane < ROPE_DIM
    c_ref[...] = jnp.where(in_rope, cos, 1.0)
    s_ref[...] = jnp.where(lane < ROPE_HALF, -sin, jnp.where(in_rope, sin, 0.0))


def _rope_tables(positions):
    B, S = positions.shape
    spec = pl.BlockSpec((None, S, LANES), lambda b: (b, 0, 0))
    shape = jax.ShapeDtypeStruct((B, S, LANES), F32)
    return pl.pallas_call(
        _rope_table_kernel,
        grid=(B,),
        in_specs=[pl.BlockSpec((None, S, 1), lambda b: (b, 0, 0))],
        out_specs=[spec, spec],
        out_shape=[shape, shape],
        name="rope_tables",
    )(positions.reshape(B, S, 1))


def _norm_matmul_kernel(x_ref, g_ref, w_ref, o_ref, h_ref):
    @pl.when(pl.program_id(1) == 0)
    def _():
        x = x_ref[...]
        h_ref[...] = ((x * _rms_scale(x)) * g_ref[...]).astype(BF16)

    o_ref[...] = jnp.dot(h_ref[...], w_ref[...], preferred_element_type=F32)


def _norm_matmul(x, gain, w, layer, *, tm=1024, tn=1920):
    T, D = x.shape
    N = w.shape[-1]
    return pl.pallas_call(
        _norm_matmul_kernel,
        grid=(T // tm, N // tn),
        in_specs=[
            pl.BlockSpec((tm, D), lambda i, j: (i, 0)),
            pl.BlockSpec((None, 1, D), lambda i, j: (layer, 0, 0)),
            pl.BlockSpec((None, D, tn), lambda i, j: (layer, 0, j)),
        ],
        out_specs=pl.BlockSpec((tm, tn), lambda i, j: (i, j)),
        out_shape=jax.ShapeDtypeStruct((T, N), F32),
        scratch_shapes=[pltpu.VMEM((tm, D), BF16)],
        compiler_params=pltpu.CompilerParams(
            dimension_semantics=("parallel", "arbitrary"), vmem_limit_bytes=VMEM_LIMIT),
        name="norm_in_proj",
    )(x, gain, w)


def _split_bf16(t):
    hi = t.astype(BF16)
    return hi, (t - hi.astype(F32)).astype(BF16)


def _moba_kernel(q_ref, k_ref, v_ref, c_ref, s_ref, o_ref,
                 kr_ref, vt_ref, qa_ref, bias_ref, *, n_blocks):
    i = pl.program_id(2)
    blk = MOBA_BLOCK

    def swap_halves(t):
        src = lax.broadcasted_iota(jnp.int32, (LANES, LANES), 0)
        dst = lax.broadcasted_iota(jnp.int32, (LANES, LANES), 1)
        swap = jnp.logical_or(jnp.logical_and(dst < ROPE_HALF, src == dst + ROPE_HALF),
                              jnp.logical_and(jnp.logical_and(dst >= ROPE_HALF, dst < ROPE_DIM),
                                              src == dst - ROPE_HALF)).astype(BF16)
        return jnp.dot(jnp.concatenate(_split_bf16(t), axis=1), jnp.concatenate([swap, swap], axis=0),
                       preferred_element_type=F32)

    def rope(t, swapped, rows):
        return t * c_ref[rows, :] + swapped * s_ref[rows, :]

    heads = range(kr_ref.shape[0])
    cols = lambda hh: slice(hh * HEAD_DIM, (hh + 1) * HEAD_DIM)

    @pl.when(i == 0)
    def _():
        for hh in heads:
            k_means = []
            k_swapped = swap_halves(k_ref[:, cols(hh)])
            q_swapped = swap_halves(q_ref[:, cols(hh)])
            for b in range(n_blocks):
                rows = pl.ds(b * blk, blk)
                kb = rope(k_ref[rows, cols(hh)], k_swapped[b * blk:(b + 1) * blk, :], rows)
                kr_ref[hh, rows, :] = kb.astype(BF16)
                k_means.append(jnp.mean(kb, axis=0, keepdims=True))
                vt_ref[hh, :, b * blk:(b + 1) * blk] = v_ref[rows, cols(hh)].astype(BF16).T
            km_hi, km_lo = _split_bf16(jnp.concatenate(k_means, axis=0))
            km_cat = jnp.concatenate([km_hi, km_lo, km_hi], axis=1)
            for qb in range(n_blocks):
                rows = pl.ds(qb * blk, blk)
                q = rope(q_ref[rows, cols(hh)], q_swapped[qb * blk:(qb + 1) * blk, :], rows) * (HEAD_DIM ** -0.5)
                q_hi, q_lo = _split_bf16(q)
                gate = lax.dot_general(km_cat, jnp.concatenate([q_hi, q_hi, q_lo], axis=1),
                                       NT_DIMS, preferred_element_type=F32)
                jidx = lax.broadcasted_iota(jnp.int32, gate.shape, 0)
                past = jidx < qb
                gate = jnp.where(past, gate, -jnp.inf)
                rank = jnp.zeros(gate.shape, jnp.int32)
                for m in range(n_blocks):
                    gm = gate[m:m + 1, :]
                    beats = jnp.logical_or(gm > gate, jnp.logical_and(gm == gate, m < jidx))
                    rank = rank + beats.astype(jnp.int32)
                selected = jnp.logical_and(past, rank < MOBA_TOPK)
                bias_ref[hh, qb] = jnp.where(selected, 0.0, -jnp.inf)
                qa_ref[hh, qb] = (q * LOG2E).astype(BF16)

    q_atts = [qa_ref[hh, i] for hh in heads]
    biases = [bias_ref[hh, i] for hh in heads]

    key_i = lax.broadcasted_iota(jnp.int32, (blk, blk), 0)
    qry_i = lax.broadcasted_iota(jnp.int32, (blk, blk), 1)
    causal_bias = jnp.where(key_i <= qry_i, 0.0, -jnp.inf)

    def attend(n):
        nk = (n + 1) * blk
        for hh in heads:
            bias = biases[hh]
            s = lax.dot_general(kr_ref[hh, 0:nk, :], q_atts[hh], NT_DIMS,
                                preferred_element_type=F32)
            parts = [s[j * blk:(j + 1) * blk, :] + bias[j:j + 1, :] for j in range(n)]
            parts.append(s[n * blk:nk, :] + causal_bias)
            m_max = functools.reduce(jnp.maximum, [jnp.max(t, axis=0, keepdims=True) for t in parts])
            probs = [jnp.exp2(t - m_max) for t in parts]
            denom = functools.reduce(jnp.add, [jnp.sum(t, axis=0, keepdims=True) for t in probs])
            acc = jnp.dot(vt_ref[hh, :, 0:nk], jnp.concatenate(probs, axis=0).astype(BF16),
                          preferred_element_type=F32)
            o_ref[:, cols(hh)] = (acc * (1.0 / denom)).T

    for n in range(n_blocks):
        pl.when(i == n)(functools.partial(attend, n))


def _moba_attention(proj, c_tab, s_tab):
    B, S, _ = proj.shape
    nb = S // MOBA_BLOCK
    hp = MOBA_HEADS_PER_STEP
    G = ATTN_HEADS // hp
    width = hp * HEAD_DIM
    full = lambda off: pl.BlockSpec((None, S, width), lambda b, h, i: (b, 0, off + h))
    tab = pl.BlockSpec((None, S, LANES), lambda b, h, i: (b, 0, 0))
    return pl.pallas_call(
        functools.partial(_moba_kernel, n_blocks=nb),
        grid=(B, G, nb),
        in_specs=[full(0), full(G), full(2 * G), tab, tab],
        out_specs=pl.BlockSpec((None, MOBA_BLOCK, width), lambda b, h, i: (b, i, h)),
        out_shape=jax.ShapeDtypeStruct((B, S, ATTN_WIDTH), F32),
        scratch_shapes=[
            pltpu.VMEM((hp, S, HEAD_DIM), BF16),
            pltpu.VMEM((hp, HEAD_DIM, S), BF16),
            pltpu.VMEM((hp, nb, MOBA_BLOCK, HEAD_DIM), BF16),
            pltpu.VMEM((hp, nb, nb, MOBA_BLOCK), F32),
        ],
        compiler_params=pltpu.CompilerParams(
            dimension_semantics=("parallel", "parallel", "arbitrary"), vmem_limit_bytes=VMEM_LIMIT),
        name="moba_attention",
    )(proj, proj, proj, c_tab, s_tab)


def _ssd_kernel(z_ref, xs_ref, bc_ref, dt_ref, wx_ref, bx_ref, wbc_ref, bbc_ref,
                dtb_ref, alog_ref, dskip_ref, gn_ref, o_ref,
                extx_ref, extbc_ref, state_ref):
    L = SSD_CHUNK
    N = SSD_STATE
    halo = SUBLANES

    @pl.when(pl.program_id(1) == 0)
    def _():
        extx_ref[0:halo, :] = jnp.zeros((halo, SSD_WIDTH), F32)
        extbc_ref[0:halo, :] = jnp.zeros((halo, SSD_BC), F32)
        state_ref[...] = jnp.zeros(state_ref.shape, F32)

    def chunk(rows):
        def conv_silu(ext_ref, cur_ref, w_ref, b_ref):
            ext_ref[halo:halo + L, :] = cur_ref[rows, :]
            acc = b_ref[...] + ext_ref[pl.ds(halo, L), :] * w_ref[SSD_CONV - 1:SSD_CONV, :]
            for k in range(SSD_CONV - 1):
                acc = acc + ext_ref[pl.ds(halo - (SSD_CONV - 1) + k, L), :] * w_ref[k:k + 1, :]
            ext_ref[0:halo, :] = ext_ref[L:L + halo, :]
            return _silu(acc)

        xc = conv_silu(extx_ref, xs_ref, wx_ref, bx_ref)
        bcc = conv_silu(extbc_ref, bc_ref, wbc_ref, bbc_ref)

        dt_in = dt_ref[rows, :] + dtb_ref[...]
        dt = jnp.maximum(dt_in, 0.0) + jnp.log(1.0 + jnp.exp(-jnp.abs(dt_in)))
        a_dt = dt * (-jnp.exp(alog_ref[...]))
        row = lax.broadcasted_iota(jnp.int32, (L, L), 0)
        col = lax.broadcasted_iota(jnp.int32, (L, L), 1)
        tri = row >= col
        a_cs = jnp.dot(tri.astype(F32), a_dt, precision=lax.Precision.HIGHEST,
                       preferred_element_type=F32)
        a_cs_t = a_cs.T[0:SSD_HEADS, :]
        dt_t = dt.T[0:SSD_HEADS, :]
        acs_mlogdt_t = a_cs_t - jnp.log(dt_t)
        a_last = a_cs_t[:, L - 1:L]
        w_t = dt_t * jnp.exp(a_last - a_cs_t)
        chunk_decay = jnp.broadcast_to(jnp.exp(a_last), (SSD_HEADS, LANES))

        lane = lax.broadcasted_iota(jnp.int32, (1, LANES), 1)
        lo = lane < SSD_HEAD_DIM

        def block_diag(v):
            return jnp.concatenate([jnp.where(lo, v, 0.0), jnp.where(lo, 0.0, v)], axis=0).astype(BF16)

        heads_per_group = SSD_HEADS // SSD_GROUPS
        y_parts = []
        for g in range(SSD_GROUPS):
            b_g = bcc[:, g * N:(g + 1) * N]
            c_g = bcc[:, (SSD_GROUPS + g) * N:(SSD_GROUPS + g + 1) * N]
            cb = lax.dot_general(c_g.astype(BF16), b_g.astype(BF16), NT_DIMS,
                                 preferred_element_type=F32)
            b_t = b_g.T
            for pr in range(heads_per_group // 2):
                m_parts, cs_parts, bw_parts = [], [], []
                for e in (g * heads_per_group + 2 * pr, g * heads_per_group + 2 * pr + 1):
                    colb = jnp.broadcast_to(a_cs[:, e:e + 1], (L, L))
                    decay_dt = jnp.where(tri, jnp.exp(colb - acs_mlogdt_t[e:e + 1, :]), 0.0)
                    m_parts.append((cb * decay_dt).astype(BF16))
                    cs_parts.append((c_g * jnp.exp(colb)).astype(BF16))
                    bw_parts.append((b_t * w_t[e:e + 1, :]).astype(BF16))
                pair = g * (heads_per_group // 2) + pr
                e0 = 2 * pair
                x_bd = block_diag(xc[:, pair * LANES:(pair + 1) * LANES])
                st = state_ref[pair]
                rhs = jnp.concatenate([x_bd, block_diag(st)], axis=0)
                y_parts.append(jnp.dot(jnp.concatenate(m_parts + cs_parts, axis=1), rhs,
                                       preferred_element_type=F32))
                st_new = jnp.dot(jnp.concatenate(bw_parts, axis=1), x_bd,
                                 preferred_element_type=F32)
                dec = jnp.where(lo, chunk_decay[e0:e0 + 1, :], chunk_decay[e0 + 1:e0 + 2, :])
                state_ref[pair] = st * dec + st_new

        y = jnp.concatenate(y_parts, axis=1) + dskip_ref[...] * xc
        y = y * _silu(z_ref[rows, :])
        gw = SSD_WIDTH // SSD_GROUPS
        outs = []
        for g in range(SSD_GROUPS):
            yg = y[:, g * gw:(g + 1) * gw]
            outs.append(yg * _rms_scale(yg) * gn_ref[:, g * gw:(g + 1) * gw])
        o_ref[rows, :] = jnp.concatenate(outs, axis=1)

    for sub in range(o_ref.shape[0] // L):
        chunk(pl.ds(sub * L, L))


def _ssd(proj, conv_w, conv_b, dt_bias, a_log, d_skip, gn, layer):
    B, S, _ = proj.shape
    L = SSD_CHUNKS_PER_STEP * SSD_CHUNK
    col = lambda width, off: pl.BlockSpec((None, L, width), lambda b, c: (b, c, off // width))
    par = lambda rows, width, off: pl.BlockSpec((None, rows, width), lambda b, c: (layer, 0, off // width))
    return pl.pallas_call(
        _ssd_kernel,
        grid=(B, S // L),
        in_specs=[
            col(SSD_WIDTH, OFF_Z), col(SSD_WIDTH, OFF_X), col(SSD_BC, OFF_BC), col(LANES, OFF_DT),
            par(SSD_CONV, SSD_WIDTH, 0), par(1, SSD_WIDTH, 0),
            par(SSD_CONV, SSD_BC, SSD_WIDTH), par(1, SSD_BC, SSD_WIDTH),
            par(1, LANES, 0), par(1, LANES, 0), par(1, SSD_WIDTH, 0), par(1, SSD_WIDTH, 0),
        ],
        out_specs=pl.BlockSpec((None, L, SSD_WIDTH), lambda b, c: (b, c, 0)),
        out_shape=jax.ShapeDtypeStruct((B, S, SSD_WIDTH), F32),
        scratch_shapes=[
            pltpu.VMEM((SSD_CHUNK + SUBLANES, SSD_WIDTH), F32),
            pltpu.VMEM((SSD_CHUNK + SUBLANES, SSD_BC), F32),
            pltpu.VMEM((SSD_HEADS // 2, SSD_STATE, LANES), F32),
        ],
        compiler_params=pltpu.CompilerParams(
            dimension_semantics=("parallel", "arbitrary"), vmem_limit_bytes=VMEM_LIMIT),
        name="ssd_scan",
    )(proj, proj, proj, proj, conv_w, conv_b, conv_w, conv_b, dt_bias, a_log, d_skip, gn)


def _out_proj_kernel(a_ref, y_ref, x_ref, ga_ref, gp_ref, wf_ref, o_ref, w_ref):
    @pl.when(pl.program_id(0) == 0)
    def _():
        w_ref[...] = wf_ref[...].astype(BF16)

    a = a_ref[...]
    an = (a * _rms_scale(a) * ga_ref[...]).astype(BF16)
    o = jnp.dot(an, w_ref[0:ATTN_WIDTH, :], preferred_element_type=F32)
    o = o + jnp.dot(y_ref[...].astype(BF16), w_ref[ATTN_WIDTH:, :], preferred_element_type=F32)
    o_ref[...] = x_ref[...] + o * _rms_scale(o) * gp_ref[...]


def _out_proj(attn, y, x, g_attn, g_post, w, layer, *, tm=512):
    T, D = x.shape
    vec = lambda width: pl.BlockSpec((None, 1, width), lambda i: (layer, 0, 0))
    return pl.pallas_call(
        _out_proj_kernel,
        grid=(T // tm,),
        in_specs=[
            pl.BlockSpec((tm, ATTN_WIDTH), lambda i: (i, 0)),
            pl.BlockSpec((tm, SSD_WIDTH), lambda i: (i, 0)),
            pl.BlockSpec((tm, D), lambda i: (i, 0)),
            vec(ATTN_WIDTH), vec(D),
            pl.BlockSpec((None, ATTN_WIDTH + SSD_WIDTH, D), lambda i: (layer, 0, 0),
                         pipeline_mode=pl.Buffered(1)),
        ],
        out_specs=pl.BlockSpec((tm, D), lambda i: (i, 0)),
        out_shape=jax.ShapeDtypeStruct((T, D), F32),
        scratch_shapes=[pltpu.VMEM((ATTN_WIDTH + SSD_WIDTH, D), BF16)],
        compiler_params=pltpu.CompilerParams(
            dimension_semantics=("arbitrary",), vmem_limit_bytes=VMEM_LIMIT),
        name="out_proj",
    )(attn, y, x, g_attn, g_post, w)


def _mlp_kernel(x_ref, g1_ref, g2_ref, wu_ref, wd_ref, o_ref, h_ref, acc_ref):
    f = pl.program_id(1)

    @pl.when(f == 0)
    def _():
        x = x_ref[...]
        h_ref[...] = ((x * _rms_scale(x)) * g1_ref[...]).astype(BF16)
        acc_ref[...] = jnp.zeros(acc_ref.shape, F32)

    u = jnp.maximum(jnp.dot(h_ref[...], wu_ref[...], preferred_element_type=F32), 0.0)
    acc_ref[...] += jnp.dot((u * u).astype(BF16), wd_ref[...], preferred_element_type=F32)

    @pl.when(f == pl.num_programs(1) - 1)
    def _():
        acc = acc_ref[...]
        o_ref[...] = x_ref[...] + acc * _rms_scale(acc) * g2_ref[...]


def _mlp(x, g_pre, g_post, w_up, w_down, layer, *, tm=512, tf=1024):
    T, D = x.shape
    F = w_up.shape[-1]
    vec = pl.BlockSpec((None, 1, D), lambda i, f: (layer, 0, 0))
    return pl.pallas_call(
        _mlp_kernel,
        grid=(T // tm, F // tf),
        in_specs=[
            pl.BlockSpec((tm, D), lambda i, f: (i, 0)),
            vec, vec,
            pl.BlockSpec((None, D, tf), lambda i, f: (layer, 0, f)),
            pl.BlockSpec((None, tf, D), lambda i, f: (layer, f, 0)),
        ],
        out_specs=pl.BlockSpec((tm, D), lambda i, f: (i, 0)),
        out_shape=jax.ShapeDtypeStruct((T, D), F32),
        scratch_shapes=[pltpu.VMEM((tm, D), BF16), pltpu.VMEM((tm, D), F32)],
        compiler_params=pltpu.CompilerParams(
            dimension_semantics=("parallel", "arbitrary"), vmem_limit_bytes=VMEM_LIMIT),
        name="mlp",
    )(x, g_pre, g_post, w_up, w_down)


def _row(p):
    return p[:, None, :]


def _pad_lanes(p):
    return jnp.pad(p, ((0, 0), (0, LANES - p.shape[-1])))


def kernel(x, positions, pre_mix_norm, post_mix_norm, pre_mlp_norm, post_mlp_norm, w_in, conv_w, conv_b, dt_bias, a_log, d_skip, attn_out_norm, ssd_out_norm, w_out, w_up, w_down):
    B, S, D = x.shape
    depth = w_in.shape[0]
    T = B * S

    w_in_b = lax.dynamic_update_slice(jnp.zeros((depth, D, IN_PAD), BF16), w_in.astype(BF16), (0, 0, 0))
    w_up_b = w_up.astype(BF16)
    w_down_b = w_down.astype(BF16)
    dt_bias_p = _row(_pad_lanes(dt_bias))
    a_log_p = _row(_pad_lanes(a_log))
    d_skip_x = _row(jnp.repeat(d_skip, SSD_HEAD_DIM, axis=-1))
    conv_b_r = _row(conv_b)
    g_pre_mix, g_post_mix = _row(pre_mix_norm), _row(post_mix_norm)
    g_pre_mlp, g_post_mlp = _row(pre_mlp_norm), _row(post_mlp_norm)
    g_attn, g_ssd = _row(attn_out_norm), _row(ssd_out_norm)

    c_tab, s_tab = _rope_tables(positions)

    xf = x.reshape(T, D)
    for l in range(depth):
        proj = _norm_matmul(xf, g_pre_mix, w_in_b, l).reshape(B, S, IN_PAD)
        attn = _moba_attention(proj, c_tab, s_tab)
        y = _ssd(proj, conv_w, conv_b_r, dt_bias_p, a_log_p, d_skip_x, g_ssd, l)
        xf = _out_proj(attn.reshape(T, ATTN_WIDTH), y.reshape(T, SSD_WIDTH), xf,
                       g_attn, g_post_mix, w_out, l)
        xf = _mlp(xf, g_pre_mlp, g_post_mlp, w_up_b, w_down_b, l)
    return xf.reshape(B, S, D)
```
